```python
import functools
import jax, jax.numpy as jnp
from jax import lax
import numpy as np

D_MODEL = 2048
BATCH = 1
SEQ = 8192
DEPTH = 2
DEC_BATCH = 128
DEC_SEQ = 8
PAST_LEN = 2048
PAGE_SIZE = 128

HEAD_W = 128
N_MIX_HEADS = D_MODEL // HEAD_W
A_HEADS = N_MIX_HEADS // 4
B_HEADS = N_MIX_HEADS // 2
C_GROUPS = N_MIX_HEADS - A_HEADS - B_HEADS
D_A = A_HEADS * HEAD_W
D_B = B_HEADS * HEAD_W
D_C = C_GROUPS * HEAD_W
D_MIX = D_A + D_B + D_C
CHUNK = 128
Q_BLOCK = 128
CONV_W = 3
N_EXPERTS = 16
N_GROUPS = 4
EXPERTS_PER_GROUP = N_EXPERTS // N_GROUPS
TOP_K = 2
D_FF = D_MODEL // 4
D_PLE = 256
ALPHA = (2 * DEPTH) ** 0.25
BETA = (8 * DEPTH) ** -0.25
LN_EPS = 1e-5
RMS_EPS = 1e-6
ATTN_SCALE = HEAD_W ** -0.5
SPLIT_SIZES = [D_A, D_A, D_B, D_B, D_B, B_HEADS, D_C, D_C, D_C]
D_IN = sum(SPLIT_SIZES)

kernel_name = "hymba_fox_gmlp_conv_moe_step"


def layer_norm(x, g, b=None):
    xf = x.astype(jnp.float32)
    mu = jnp.mean(xf, -1, keepdims=True)
    var = jnp.mean(jnp.square(xf - mu), -1, keepdims=True)
    y = ((xf - mu) * lax.rsqrt(var + LN_EPS)).astype(x.dtype) * g
    return y if b is None else y + b


def head_rms(y):
    yf = y.astype(jnp.float32).reshape(*y.shape[:-1], N_MIX_HEADS, HEAD_W)
    yf = yf * lax.rsqrt(jnp.mean(yf * yf, -1, keepdims=True) + RMS_EPS)
    return yf.reshape(y.shape).astype(y.dtype)


def split_cols(z):
    idx = np.cumsum(SPLIT_SIZES)[:-1].tolist()
    return jnp.split(z, idx, axis=-1)


def spatial_gate(u, v, w_s, b_s):
    n, t, _ = v.shape
    L = min(t, CHUNK)
    vc = v.reshape(n * (t // L), L, A_HEADS, HEAD_W)
    w = jnp.tril(w_s[:, :L, :L])
    mix = jnp.einsum('hts,nshc->nthc', w, vc) + b_s[:, :L].T[None, :, :, None]
    return u * mix.reshape(n, t, D_A)


def fox_prompt(q, k, v, logf):
    b, s, h, dh = q.shape
    nb = s // Q_BLOCK
    c = jnp.cumsum(logf, axis=1).transpose(0, 2, 1)
    qb = q.reshape(b, nb, Q_BLOCK, h, dh).transpose(1, 0, 2, 3, 4)
    cqb = c.reshape(b, h, nb, Q_BLOCK).transpose(2, 0, 1, 3)
    pos_k = jnp.arange(s)

    def one_block(args):
        i, q_i, c_i = args
        sc = jnp.einsum('bqhd,bkhd->bhqk', q_i, k).astype(jnp.float32) * ATTN_SCALE
        sc = sc + c_i[..., :, None] - c[:, :, None, :]
        pos_q = i * Q_BLOCK + jnp.arange(Q_BLOCK)
        sc = jnp.where(pos_k[None, :] <= pos_q[:, None], sc, -jnp.inf)
        p = jax.nn.softmax(sc, axis=-1).astype(v.dtype)
        return jnp.einsum('bhqk,bkhd->bqhd', p, v)

    o = lax.map(one_block, (jnp.arange(nb), qb, cqb))
    return o.transpose(1, 0, 2, 3, 4).reshape(b, s, h * dh)


def fox_sample(q, k, v, logf, k_past, v_past, logf_past):
    n, t, h, dh = q.shape
    p_len = k_past.shape[1]
    k_all = jnp.concatenate([k_past.astype(k.dtype), k], axis=1)
    v_all = jnp.concatenate([v_past.astype(v.dtype), v], axis=1)
    c = jnp.cumsum(jnp.concatenate([logf_past.astype(jnp.float32), logf], axis=1), axis=1).transpose(0, 2, 1)
    c_q = c[:, :, p_len:]
    sc = jnp.einsum('bqhd,bkhd->bhqk', q, k_all).astype(jnp.float32) * ATTN_SCALE
    sc = sc + c_q[..., :, None] - c[:, :, None, :]
    mask = jnp.arange(p_len + t)[None, :] <= (p_len + jnp.arange(t))[:, None]
    sc = jnp.where(mask, sc, -jnp.inf)
    p = jax.nn.softmax(sc, axis=-1).astype(v.dtype)
    return jnp.einsum('bhqk,bkhd->bqhd', p, v_all).reshape(n, t, h * dh)


def mixer_block(x, w_in, b_f, g_v, w_s, b_s, conv_w, conv_buf, g_mix, w_out, attend):
    n, t, _ = x.shape
    a_u, a_v, q, k, v, f_logit, c_b, c_c, c_h = split_cols(x @ w_in)
    a_v = layer_norm(jax.nn.gelu(a_v), g_v)
    o_a = spatial_gate(jax.nn.gelu(a_u), a_v, w_s, b_s)
    heads = lambda y: y.reshape(n, t, B_HEADS, HEAD_W)
    q, k, v = heads(q), heads(k), heads(v)
    logf = jax.nn.log_sigmoid((f_logit + b_f).astype(jnp.float32))
    o_b = attend(q, k, v, logf)
    zc = c_c * c_h
    zp = jnp.concatenate([conv_buf.astype(zc.dtype), zc], axis=1)
    y_c = sum(conv_w[j] * zp[:, j:j + t] for j in range(CONV_W))
    o_c = c_b * y_c
    mixed = head_rms(jnp.concatenate([o_a, o_b, o_c], axis=-1)) * g_mix
    return mixed @ w_out, (a_v, k, v, logf, zp[:, t:])


def route(x, w_router, b_router):
    scores = jax.nn.sigmoid((x @ w_router).astype(jnp.float32))
    sel = scores + b_router.astype(jnp.float32)
    sel_g = sel.reshape(*sel.shape[:-1], N_GROUPS, EXPERTS_PER_GROUP)
    g_score = lax.top_k(sel_g, TOP_K)[0].sum(-1)
    g_idx = jnp.argmax(g_score, axis=-1)
    sel_in = jnp.einsum('...g,...ge->...e', jax.nn.one_hot(g_idx, N_GROUPS, dtype=jnp.float32), sel_g)
    _, loc = lax.top_k(sel_in, TOP_K)
    e_idx = g_idx[..., None] * EXPERTS_PER_GROUP + loc
    w = jnp.take_along_axis(scores, e_idx, axis=-1)
    w = w / jnp.sum(w, -1, keepdims=True)
    return jnp.sum(jax.nn.one_hot(e_idx, N_EXPERTS, dtype=jnp.float32) * w[..., None], axis=-2)


def moe(x, gates, w_gate, w_up, w_down):
    h = jax.nn.silu(jnp.einsum('...d,edf->...ef', x, w_gate)) * jnp.einsum('...d,edf->...ef', x, w_up)
    h = h * gates[..., None].astype(h.dtype)
    return jnp.einsum('...ef,efd->...d', h, w_down)


def channel_and_embed(x, mix, p, ln1_g, ln1_b, w_router, b_router, w_gate, w_up, w_down,
                      ln2_g, ln2_b, w_pg, b_pg, w_pp):
    x = layer_norm(ALPHA * x + mix, ln1_g, ln1_b)
    gates = route(x, w_router, b_router)
    x = layer_norm(ALPHA * x + moe(x, gates, w_gate, w_up, w_down), ln2_g, ln2_b)
    return x + jax.nn.sigmoid(x @ w_pg + b_pg) * (p @ w_pp)


def setup_inputs(seed: int = 0) -> dict:
    key = jax.random.key(seed)
    ks = iter(jax.random.split(key, 40))
    nrm = lambda shape, scale: jax.random.normal(next(ks), shape, jnp.float32) * scale
    n_pages = PAST_LEN // PAGE_SIZE
    n_used = DEC_BATCH * n_pages
    n_pool = n_used + n_used // 4
    page_table = jax.random.permutation(next(ks), n_pool)[:n_used].reshape(DEC_BATCH, n_pages).astype(jnp.int32)
    return {
        "x_prompt": nrm((BATCH, SEQ, D_MODEL), 1.0),
        "x_sample": nrm((DEC_BATCH, DEC_SEQ, D_MODEL), 1.0),
        "p_prompt": nrm((DEPTH, BATCH, SEQ, D_PLE), 1.0),
        "p_sample": nrm((DEPTH, DEC_BATCH, DEC_SEQ, D_PLE), 1.0),
        "cache_k": nrm((DEPTH, n_pool, PAGE_SIZE, B_HEADS, HEAD_W), 1.0),
        "cache_v": nrm((DEPTH, n_pool, PAGE_SIZE, B_HEADS, HEAD_W), 1.0),
        "cache_logf": jax.nn.log_sigmoid(3.0 + nrm((DEPTH, n_pool, PAGE_SIZE, B_HEADS), 1.0)),
        "state_conv": nrm((DEPTH, DEC_BATCH, CONV_W - 1, D_C), 1.0),
        "page_table": page_table,
        "w_in": nrm((DEPTH, D_MODEL, D_IN), D_MODEL ** -0.5),
        "b_f": 3.0 + nrm((DEPTH, B_HEADS), 0.1),
        "g_v": 1.0 + nrm((DEPTH, D_A), 0.02),
        "w_s": nrm((DEPTH, A_HEADS, CHUNK, CHUNK), 0.5 * CHUNK ** -0.5),
        "b_s": 1.0 + nrm((DEPTH, A_HEADS, CHUNK), 0.02),
        "conv_w": nrm((DEPTH, CONV_W, D_C), CONV_W ** -0.5),
        "g_mix": 1.0 + nrm((DEPTH, D_MIX), 0.02),
        "w_out": nrm((DEPTH, D_MIX, D_MODEL), BETA * D_MIX ** -0.5),
        "ln1_g": 1.0 + nrm((DEPTH, D_MODEL), 0.02),
        "ln1_b": nrm((DEPTH, D_MODEL), 0.02),
        "w_router": nrm((D_MODEL, N_EXPERTS), D_MODEL ** -0.5),
        "b_router": nrm((N_EXPERTS,), 0.01),
        "w_gate": nrm((DEPTH, N_EXPERTS, D_MODEL, D_FF), D_MODEL ** -0.5),
        "w_up": nrm((DEPTH, N_EXPERTS, D_MODEL, D_FF), D_MODEL ** -0.5),
        "w_down": nrm((DEPTH, N_EXPERTS, D_FF, D_MODEL), BETA * D_FF ** -0.5),
        "ln2_g": 1.0 + nrm((DEPTH, D_MODEL), 0.02),
        "ln2_b": nrm((DEPTH, D_MODEL), 0.02),
        "w_ple_gate": nrm((DEPTH, D_MODEL, D_MODEL), D_MODEL ** -0.5),
        "b_ple_gate": nrm((DEPTH, D_MODEL), 0.02),
        "w_ple_proj": nrm((DEPTH, D_PLE, D_MODEL), BETA * D_PLE ** -0.5),
    }


def reference(x_prompt, x_sample, p_prompt, p_sample, cache_k, cache_v, cache_logf, state_conv, page_table,
              w_in, b_f, g_v, w_s, b_s, conv_w, g_mix, w_out, ln1_g, ln1_b, w_router, b_router,
              w_gate, w_up, w_down, ln2_g, ln2_b, w_ple_gate, b_ple_gate, w_ple_proj):
    n_seq, n_pages = page_table.shape
    past = n_pages * PAGE_SIZE
    xp, xs = x_prompt, x_sample
    kp, vp, lfp, cvp = [], [], [], []
    avs, kss, vss, lfs, cvs = [], [], [], [], []
    for i in range(DEPTH):
        mixer_w = (w_in[i], b_f[i], g_v[i], w_s[i], b_s[i], conv_w[i])
        chan_w = (ln1_g[i], ln1_b[i], w_router, b_router, w_gate[i], w_up[i], w_down[i],
                  ln2_g[i], ln2_b[i], w_ple_gate[i], b_ple_gate[i], w_ple_proj[i])
        zero_buf = jnp.zeros((xp.shape[0], CONV_W - 1, D_C), xp.dtype)
        mix_p, (_, k_p, v_p, lf_p, cv_p) = mixer_block(xp, *mixer_w, zero_buf, g_mix[i], w_out[i], fox_prompt)
        xp = channel_and_embed(xp, mix_p, p_prompt[i], *chan_w)
        kp.append(k_p); vp.append(v_p); lfp.append(lf_p); cvp.append(cv_p)
        k_past = cache_k[i, page_table].reshape(n_seq, past, B_HEADS, HEAD_W)
        v_past = cache_v[i, page_table].reshape(n_seq, past, B_HEADS, HEAD_W)
        lf_past = cache_logf[i, page_table].reshape(n_seq, past, B_HEADS)
        attend_s = functools.partial(fox_sample, k_past=k_past, v_past=v_past, logf_past=lf_past)
        mix_s, (av_s, k_s, v_s, lf_s, cv_s) = mixer_block(xs, *mixer_w, state_conv[i], g_mix[i], w_out[i], attend_s)
        xs = channel_and_embed(xs, mix_s, p_sample[i], *chan_w)
        avs.append(av_s); kss.append(k_s); vss.append(v_s); lfs.append(lf_s); cvs.append(cv_s)
    return (xp, xs, jnp.stack(kp), jnp.stack(vp), jnp.stack(lfp), jnp.stack(cvp),
            jnp.stack(avs), jnp.stack(kss), jnp.stack(vss), jnp.stack(lfs), jnp.stack(cvs))
```

```python
import functools

import numpy as np
import jax
import jax.numpy as jnp
from jax import lax
from jax.experimental import pallas as pl
from jax.experimental.pallas import tpu as pltpu

F32 = jnp.float32
BF16 = jnp.bfloat16

HEAD_W = 128
CHUNK = 128
CONV_W = 3
N_GROUPS = 4
EXPERTS_PER_GROUP = 4
N_EXPERTS = N_GROUPS * EXPERTS_PER_GROUP
LN_EPS = 1e-5
RMS_EPS = 1e-6
ATTN_SCALE = HEAD_W ** -0.5
LANES = 128
SUBLANES = 8
MIB = 1024 * 1024


def _params(n_axes, vmem_mib):
    return pltpu.CompilerParams(dimension_semantics=("arbitrary",) * n_axes,
                                vmem_limit_bytes=vmem_mib * MIB)


def _resident(shape):
    zeros = (0,) * len(shape)
    return pl.BlockSpec(shape, lambda *_: zeros, pipeline_mode=pl.Buffered(1))


def _layer_norm(v, gain, bias=None):
    mu = jnp.mean(v, -1, keepdims=True)
    var = jnp.mean(jnp.square(v - mu), -1, keepdims=True)
    y = (v - mu) * lax.rsqrt(var + LN_EPS) * gain
    return y if bias is None else y + bias


def _head_rms(o):
    return o * lax.rsqrt(jnp.mean(o * o, -1, keepdims=True) + RMS_EPS)


def _split3(v):
    hi = v.astype(BF16)
    r1 = v - hi.astype(F32)
    mid = r1.astype(BF16)
    lo = (r1 - mid.astype(F32)).astype(BF16)
    return hi, mid, lo


def _proj_kernel(x_ref, w_ref, *out_refs):
    z = jnp.dot(x_ref[...].astype(BF16), w_ref[...], preferred_element_type=F32)
    for o_ref in out_refs:
        if len(o_ref.shape) == 3:
            for h in range(o_ref.shape[1]):
                o_ref[:, h, :] = z[:, h * HEAD_W:(h + 1) * HEAD_W].astype(o_ref.dtype)
        else:
            o_ref[...] = z.astype(o_ref.dtype)


def _project(x, w, *, tm, flat_dtypes=(), heads_f32=False, sample_f32_rows=None):
    r, d = x.shape
    n = w.shape[1]
    out_shape = [jax.ShapeDtypeStruct((r, n), dt) for dt in flat_dtypes]
    out_specs = [pl.BlockSpec((tm, n), lambda i: (i, 0)) for _ in flat_dtypes]
    if heads_f32:
        out_shape.append(jax.ShapeDtypeStruct((r, n // HEAD_W, HEAD_W), F32))
        out_specs.append(pl.BlockSpec((tm, n // HEAD_W, HEAD_W), lambda i: (i, 0, 0)))
    if sample_f32_rows is not None:
        first, rows = sample_f32_rows
        out_shape.append(jax.ShapeDtypeStruct((rows, n), F32))
        out_specs.append(pl.BlockSpec((tm, n), lambda i: (jnp.maximum(i - first, 0), 0)))
    return pl.pallas_call(
        _proj_kernel,
        grid=(r // tm,),
        in_specs=[pl.BlockSpec((tm, d), lambda i: (i, 0)), _resident((d, n))],
        out_specs=out_specs,
        out_shape=out_shape,
        compiler_params=_params(1, 40),
        name="proj",
    )(x, w)


def _branch_a_kernel(x_ref, w_ref, gv_ref, ws_ref, bs_ref, gm_ref, av_ref, ma_ref, *,
                     n_prompt_tiles, tm, dec_shift):
    is_sample = pl.program_id(0) >= n_prompt_tiles
    d_a = gv_ref.shape[-1]
    z = jnp.dot(x_ref[...].astype(BF16), w_ref[...], preferred_element_type=F32)
    u = jax.nn.gelu(z[:, :d_a])
    av = _layer_norm(jax.nn.gelu(z[:, d_a:]), gv_ref[...])
    av_ref[...] = av
    avb = av.astype(BF16)
    row = lax.broadcasted_iota(jnp.int32, (CHUNK, CHUNK), 0)
    col = lax.broadcasted_iota(jnp.int32, (CHUNK, CHUNK), 1)
    shift = jnp.where(is_sample, dec_shift, int(np.log2(CHUNK)))
    mask = jnp.logical_and(col <= row, jnp.right_shift(row, shift) == jnp.right_shift(col, shift))
    for h in range(d_a // HEAD_W):
        hs = slice(h * HEAD_W, (h + 1) * HEAD_W)
        wm = jnp.where(mask, ws_ref[h], 0.0).astype(BF16)
        for c in range(tm // CHUNK):
            cs = slice(c * CHUNK, (c + 1) * CHUNK)
            mix = jnp.dot(wm, avb[cs, hs], preferred_element_type=F32) + bs_ref[:, hs]
            o = _head_rms(u[cs, hs] * mix) * gm_ref[:, hs]
            ma_ref[cs, hs] = o.astype(BF16)


def _branch_a(x, w_a, g_v, ws_all, bs_all, gm_a, *, tm, n_prompt_rows, dec_seq):
    r, d = x.shape
    d_a = g_v.shape[-1]
    npt = n_prompt_rows // tm
    n_heads = d_a // HEAD_W
    kern = functools.partial(_branch_a_kernel, n_prompt_tiles=npt, tm=tm, dec_shift=int(np.log2(dec_seq)))
    sel = lambda i: (i >= npt).astype(jnp.int32)
    return pl.pallas_call(
        kern,
        grid=(r // tm,),
        in_specs=[
            pl.BlockSpec((tm, d), lambda i: (i, 0)),
            _resident((d, 2 * d_a)),
            _resident((1, d_a)),
            pl.BlockSpec((None, n_heads, CHUNK, CHUNK), lambda i: (sel(i), 0, 0, 0)),
            pl.BlockSpec((None, CHUNK, d_a), lambda i: (sel(i), 0, 0)),
            _resident((1, d_a)),
        ],
        out_specs=[
            pl.BlockSpec((tm, d_a), lambda i: (jnp.maximum(i - npt, 0), 0)),
            pl.BlockSpec((tm, d_a), lambda i: (i, 0)),
        ],
        out_shape=[
            jax.ShapeDtypeStruct((r - n_prompt_rows, d_a), F32),
            jax.ShapeDtypeStruct((r, d_a), BF16),
        ],
        compiler_params=_params(1, 32),
        name="branch_a",
    )(x, w_a, g_v, ws_all, bs_all, gm_a)


def _branch_c_kernel(x_ref, w_ref, bf_ref, cw_ref, gm_ref, s1_ref, s2_ref,
                     zc_ref, mc_ref, lf_ref, cs_ref, carry_z, carry_c, *,
                     n_prompt_tiles, tm, dec_seq):
    i = pl.program_id(0)
    d_c = gm_ref.shape[-1]

    @pl.when(i == 0)
    def _():
        carry_z[...] = jnp.zeros_like(carry_z)
        carry_c[...] = jnp.zeros_like(carry_c)

    z = jnp.dot(x_ref[...].astype(BF16), w_ref[...], preferred_element_type=F32)
    c_b = z[:, :d_c]
    zc = z[:, d_c:2 * d_c] * z[:, 2 * d_c:3 * d_c]
    zc_ref[...] = zc

    lf = jax.nn.log_sigmoid(z[:, 3 * d_c:] + bf_ref[...])
    lf_ref[...] = lf
    trow = lax.broadcasted_iota(jnp.int32, (tm, tm), 0)
    tcol = lax.broadcasted_iota(jnp.int32, (tm, tm), 1)
    tri = jnp.where(tcol <= trow, 1.0, 0.0).astype(BF16)
    hi, mid, lo = _split3(lf)
    cs = (jnp.dot(tri, hi, preferred_element_type=F32) + jnp.dot(tri, mid, preferred_element_type=F32)
          + jnp.dot(tri, lo, preferred_element_type=F32)) + carry_c[...]
    cs_ref[...] = cs
    carry_c[...] = cs[tm - 1:tm, :]

    row = lax.broadcasted_iota(jnp.int32, (tm, d_c), 0)
    back1 = pltpu.roll(zc, 1, 0)
    back2 = pltpu.roll(zc, 2, 0)

    def finish(zm1, zm2):
        y = cw_ref[0:1, :] * zm2 + cw_ref[1:2, :] * zm1 + cw_ref[2:3, :] * zc
        o = c_b * y
        for g in range(d_c // HEAD_W):
            gs = slice(g * HEAD_W, (g + 1) * HEAD_W)
            mc_ref[:, gs] = (_head_rms(o[:, gs]) * gm_ref[:, gs]).astype(BF16)

    @pl.when(i < n_prompt_tiles)
    def _():
        p6 = carry_z[SUBLANES - 2:SUBLANES - 1, :]
        p7 = carry_z[SUBLANES - 1:SUBLANES, :]
        finish(jnp.where(row >= 1, back1, p7),
               jnp.where(row >= 2, back2, jnp.where(row == 0, p6, p7)))

    @pl.when(i >= n_prompt_tiles)
    def _():
        t = jnp.bitwise_and(row, dec_seq - 1)
        finish(jnp.where(t >= 1, back1, s1_ref[...]), jnp.where(t >= 2, back2, s2_ref[...]))

    carry_z[...] = zc[tm - SUBLANES:tm, :]


def _branch_c(x, w_c, bf_pad, conv_w, gm_c, s1, s2, *, tm, n_prompt_rows, dec_seq):
    r, d = x.shape
    d_c = gm_c.shape[-1]
    n = w_c.shape[1]
    npt = n_prompt_rows // tm
    kern = functools.partial(_branch_c_kernel, n_prompt_tiles=npt, tm=tm, dec_seq=dec_seq)
    rows = lambda i: (i, 0)
    srows = lambda i: (jnp.maximum(i - npt, 0), 0)
    return pl.pallas_call(
        kern,
        grid=(r // tm,),
        in_specs=[
            pl.BlockSpec((tm, d), rows),
            _resident((d, n)),
            _resident((1, LANES)),
            _resident((CONV_W, d_c)),
            _resident((1, d_c)),
            pl.BlockSpec((tm, d_c), srows),
            pl.BlockSpec((tm, d_c), srows),
        ],
        out_specs=[
            pl.BlockSpec((tm, d_c), rows),
            pl.BlockSpec((tm, d_c), rows),
            pl.BlockSpec((tm, LANES), rows),
            pl.BlockSpec((tm, LANES), rows),
        ],
        out_shape=[
            jax.ShapeDtypeStruct((r, d_c), F32),
            jax.ShapeDtypeStruct((r, d_c), BF16),
            jax.ShapeDtypeStruct((r, LANES), F32),
            jax.ShapeDtypeStruct((r, LANES), F32),
        ],
        scratch_shapes=[pltpu.VMEM((SUBLANES, d_c), F32), pltpu.VMEM((1, LANES), F32)],
        compiler_params=_params(1, 40),
        name="branch_c",
    )(x, w_c, bf_pad, conv_w, gm_c, s1, s2)


def _prompt_attn_kernel(qi_ref, kj_ref, q_ref, k_ref, v_ref, cb_ref, gm_ref, o_ref, m_sc, l_sc, acc_sc, *, tq):
    s = pl.program_id(1)
    qi = qi_ref[s]
    kj = kj_ref[s]

    @pl.when(kj == 0)
    def _():
        m_sc[...] = jnp.full_like(m_sc, -jnp.inf)
        l_sc[...] = jnp.zeros_like(l_sc)
        acc_sc[...] = jnp.zeros_like(acc_sc)

    sc = lax.dot_general(q_ref[...], k_ref[...], (((1,), (1,)), ((), ())), preferred_element_type=F32)
    sc = sc * ATTN_SCALE + cb_ref[...]

    def update(sc):
        m_old = m_sc[...]
        m_new = jnp.maximum(m_old, jnp.max(sc, -1, keepdims=True))
        alpha = jnp.exp(m_old - m_new)
        p = jnp.exp(sc - m_new)
        l_sc[...] = alpha * l_sc[...] + jnp.sum(p, -1, keepdims=True)
        acc_sc[...] = alpha * acc_sc[...] + jnp.dot(p.astype(BF16), v_ref[...], preferred_element_type=F32)
        m_sc[...] = m_new

    @pl.when(kj < qi)
    def _():
        update(sc)

    @pl.when(kj == qi)
    def _():
        row = lax.broadcasted_iota(jnp.int32, (tq, tq), 0)
        col = lax.broadcasted_iota(jnp.int32, (tq, tq), 1)
        update(jnp.where(col <= row, sc, -jnp.inf))
        o = acc_sc[...] / l_sc[...]
        o_ref[...] = (_head_rms(o) * gm_ref[...]).astype(o_ref.dtype)


def _prompt_attention(qb, kb, vb, cneg, gm_b, *, n_prompt_rows, tq):
    n_heads = qb.shape[1] // HEAD_W
    nq = n_prompt_rows // tq
    qi = np.concatenate([np.full(i + 1, i, np.int32) for i in range(nq)])
    kj = np.concatenate([np.arange(i + 1, dtype=np.int32) for i in range(nq)])
    grid_spec = pltpu.PrefetchScalarGridSpec(
        num_scalar_prefetch=2,
        grid=(n_heads, len(qi)),
        in_specs=[
            pl.BlockSpec((tq, HEAD_W), lambda h, s, qi, kj: (qi[s], h)),
            pl.BlockSpec((tq, HEAD_W), lambda h, s, qi, kj: (kj[s], h)),
            pl.BlockSpec((tq, HEAD_W), lambda h, s, qi, kj: (kj[s], h)),
            pl.BlockSpec((None, 1, tq), lambda h, s, qi, kj: (h, 0, kj[s])),
            pl.BlockSpec((1, HEAD_W), lambda h, s, qi, kj: (0, h)),
        ],
        out_specs=pl.BlockSpec((tq, HEAD_W), lambda h, s, qi, kj: (qi[s], h)),
        scratch_shapes=[pltpu.VMEM((tq, 1), F32), pltpu.VMEM((tq, 1), F32), pltpu.VMEM((tq, HEAD_W), F32)],
    )
    return pl.pallas_call(
        functools.partial(_prompt_attn_kernel, tq=tq),
        grid_spec=grid_spec,
        out_shape=jax.ShapeDtypeStruct((n_prompt_rows, qb.shape[1]), BF16),
        compiler_params=_params(2, 32),
        name="prompt_attn",
    )(jnp.asarray(qi), jnp.asarray(kj), qb, kb, vb, cneg, gm_b)


def _sample_attn_kernel(pt_ref, q_ref, kn_ref, vn_ref, bp_ref, bn_ref, gm_ref, ck_hbm, cv_hbm, o_ref,
                        kbuf, vbuf, kb_sc, vb_sc, s_sc, p_sc, sem, *, layer, n_pages, page, dec_seq):
    n = pl.program_id(0)
    d_b = q_ref.shape[-1]
    n_heads = d_b // HEAD_W
    past = n_pages * page
    n_keys = past + LANES

    def page_copies(seq, slot):
        copies = []
        for p in range(n_pages):
            pg = pt_ref[seq, p]
            rows = pl.ds(p * page, page)
            copies.append(pltpu.make_async_copy(ck_hbm.at[layer, pg], kbuf.at[slot, rows], sem.at[0, slot]))
            copies.append(pltpu.make_async_copy(cv_hbm.at[layer, pg], vbuf.at[slot, rows], sem.at[1, slot]))
        return copies

    @pl.when(n == 0)
    def _():
        for c in page_copies(0, 0):
            c.start()

    @pl.when(n + 1 < pl.num_programs(0))
    def _():
        for c in page_copies(n + 1, (n + 1) % 2):
            c.start()

    slot = n % 2
    for c in page_copies(n, slot):
        c.wait()

    qrep = jnp.concatenate([q_ref[...]] * (LANES // dec_seq), axis=0)
    rowi = lax.broadcasted_iota(jnp.int32, (LANES, d_b), 0)
    lanei = lax.broadcasted_iota(jnp.int32, (LANES, d_b), 1)
    qbd = jnp.where(lanei // HEAD_W == rowi // dec_seq, qrep, 0.0).astype(BF16)

    pad = jnp.zeros((LANES - dec_seq, HEAD_W), F32)
    for h in range(n_heads):
        hs = slice(h * HEAD_W, (h + 1) * HEAD_W)
        kb_sc[:past, hs] = kbuf[slot, :, h, :].astype(BF16)
        vb_sc[:past, hs] = vbuf[slot, :, h, :].astype(BF16)
        kb_sc[past:, hs] = jnp.concatenate([kn_ref[:, h, :], pad], axis=0).astype(BF16)
        vb_sc[past:, hs] = jnp.concatenate([vn_ref[:, h, :], pad], axis=0).astype(BF16)

    sc = lax.dot_general(kb_sc[...], qbd, (((1,), (1,)), ((), ())), preferred_element_type=F32) * ATTN_SCALE

    def expand(b):
        col = lax.broadcasted_iota(jnp.int32, (b.shape[0], LANES), 1)
        out = jnp.zeros((b.shape[0], LANES), F32)
        for h in range(n_heads):
            out = jnp.where(col // dec_seq == h, b[:, h:h + 1], out)
        return out

    s_sc[:past, :] = sc[:past, :] + expand(bp_ref[...])
    krow = lax.broadcasted_iota(jnp.int32, (LANES, LANES), 0)
    qcol = lax.broadcasted_iota(jnp.int32, (LANES, LANES), 1) % dec_seq
    bn = jnp.concatenate([expand(bn_ref[...]), jnp.zeros((LANES - dec_seq, LANES), F32)], axis=0)
    s_sc[past:, :] = jnp.where(krow <= qcol, sc[past:, :] + bn, -jnp.inf)

    m = jnp.max(s_sc[...], axis=0, keepdims=True)
    l = jnp.zeros((LANES, 1), F32)
    for c in range(n_keys // LANES):
        cs = slice(c * LANES, (c + 1) * LANES)
        pt = jnp.exp(s_sc[cs, :] - m).T
        l = l + jnp.sum(pt, -1, keepdims=True)
        p_sc[:, cs] = pt.astype(BF16)
    acc = jnp.dot(p_sc[...], vb_sc[...], preferred_element_type=F32)
    for h in range(n_heads):
        rs = slice(h * dec_seq, (h + 1) * dec_seq)
        hs = slice(h * HEAD_W, (h + 1) * HEAD_W)
        o = acc[rs, hs] / l[rs, :]
        o_ref[:, hs] = _head_rms(o) * gm_ref[:, hs]


def _sample_attention(page_table, qs, kf, vf, bias_past, bias_new, gm_b, cache_k, cache_v, *,
                      layer, n_prompt_rows, dec_seq):
    n_seq, n_pages = page_table.shape
    _, _, page, n_heads, head_w = cache_k.shape
    d_b = n_heads * head_w
    past = n_pages * page
    first_new = n_prompt_rows // dec_seq

    new_rows = pl.BlockSpec((dec_seq, n_heads, head_w), lambda n, pt: (first_new + n, 0, 0))
    grid_spec = pltpu.PrefetchScalarGridSpec(
        num_scalar_prefetch=1,
        grid=(n_seq,),
        in_specs=[
            pl.BlockSpec((dec_seq, d_b), lambda n, pt: (n, 0)),
            new_rows,
            new_rows,
            pl.BlockSpec((None, past, n_heads), lambda n, pt: (n, 0, 0)),
            pl.BlockSpec((None, dec_seq, n_heads), lambda n, pt: (n, 0, 0)),
            pl.BlockSpec((1, d_b), lambda n, pt: (0, 0)),
            pl.BlockSpec(memory_space=pl.ANY),
            pl.BlockSpec(memory_space=pl.ANY),
        ],
        out_specs=pl.BlockSpec((dec_seq, d_b), lambda n, pt: (n, 0)),
        scratch_shapes=[
            pltpu.VMEM((2, past, n_heads, head_w), F32),
            pltpu.VMEM((2, past, n_heads, head_w), F32),
            pltpu.VMEM((past + LANES, d_b), BF16),
            pltpu.VMEM((past + LANES, d_b), BF16),
            pltpu.VMEM((past + LANES, LANES), F32),
            pltpu.VMEM((LANES, past + LANES), BF16),
            pltpu.SemaphoreType.DMA((2, 2)),
        ],
    )
    kern = functools.partial(_sample_attn_kernel, layer=layer, n_pages=n_pages, page=page, dec_seq=dec_seq)
    return pl.pallas_call(
        kern,
        grid_spec=grid_spec,
        out_shape=jax.ShapeDtypeStruct((n_seq * dec_seq, d_b), F32),
        compiler_params=_params(1, 56),
        name="sample_attn",
    )(page_table, qs, kf, vf, bias_past, bias_new, gm_b, cache_k, cache_v)


def _mix_kernel(ma_ref, mbp_ref, mbs_ref, mc_ref, x_ref, w_ref, g1_ref, b1_ref, wrh_ref, wrl_ref, br_ref,
                x1g_ref, gid_ref, mixed_sc, *, n_prompt_tiles, alpha):
    i = pl.program_id(0)
    tm, d = x_ref.shape
    d_a = ma_ref.shape[-1]
    d_b = mbp_ref.shape[-1]
    mixed_sc[:, :d_a] = ma_ref[...]
    mixed_sc[:, d_a + d_b:] = mc_ref[...]

    @pl.when(i < n_prompt_tiles)
    def _():
        mixed_sc[:, d_a:d_a + d_b] = mbp_ref[...]

    @pl.when(i >= n_prompt_tiles)
    def _():
        mixed_sc[:, d_a:d_a + d_b] = mbs_ref[...].astype(BF16)

    y = jnp.dot(mixed_sc[...], w_ref[...], preferred_element_type=F32)
    x1 = _layer_norm(alpha * x_ref[...] + y, g1_ref[...], b1_ref[...])
    x1g_ref[:, :d] = x1

    hi = x1.astype(BF16)
    lo = (x1 - hi.astype(F32)).astype(BF16)
    logits = (jnp.dot(hi, wrh_ref[...], preferred_element_type=F32)
              + jnp.dot(lo, wrh_ref[...], preferred_element_type=F32)
              + jnp.dot(hi, wrl_ref[...], preferred_element_type=F32))
    score_all = jax.nn.sigmoid(logits.T)
    sel_all = score_all + br_ref[...]
    score = [score_all[e:e + 1, :] for e in range(N_EXPERTS)]
    sel = [sel_all[e:e + 1, :] for e in range(N_EXPERTS)]

    gscore = []
    for g in range(N_GROUPS):
        a0, a1, a2, a3 = sel[4 * g:4 * g + 4]
        h1, l1 = jnp.maximum(a0, a1), jnp.minimum(a0, a1)
        h2, l2 = jnp.maximum(a2, a3), jnp.minimum(a2, a3)
        gscore.append(jnp.maximum(h1, h2) + jnp.maximum(jnp.minimum(h1, h2), jnp.maximum(l1, l2)))
    gmax = functools.reduce(jnp.maximum, gscore)
    gid = jnp.full(gscore[0].shape, N_GROUPS - 1, jnp.int32)
    for g in range(N_GROUPS - 2, -1, -1):
        gid = jnp.where(gscore[g] == gmax, g, gid)
    gid_ref[...] = gid

    chosen = []
    for e in range(N_EXPERTS):
        g = e // EXPERTS_PER_GROUP
        rank = jnp.zeros(sel[e].shape, jnp.int32)
        for o in range(4 * g, 4 * g + 4):
            if o < e:
                rank = rank + (sel[o] >= sel[e]).astype(jnp.int32)
            elif o > e:
                rank = rank + (sel[o] > sel[e]).astype(jnp.int32)
        chosen.append(jnp.logical_and(gid == g, rank < 2))
    wsum = functools.reduce(jnp.add, [jnp.where(chosen[e], score[e], 0.0) for e in range(N_EXPERTS)])
    rowi = lax.broadcasted_iota(jnp.int32, (N_EXPERTS, tm), 0)
    gates_t = jnp.zeros((N_EXPERTS, tm), F32)
    for e in range(N_EXPERTS):
        gates_t = jnp.where(rowi == e, jnp.where(chosen[e], score[e] / wsum, 0.0), gates_t)
    gates_t = jnp.concatenate([gates_t, jnp.zeros((LANES - N_EXPERTS, tm), F32)], axis=0)
    x1g_ref[:, d:] = gates_t.T


def _mix(ma, mbp, mbs, mc, x, w_out, g1, b1, wr_hi, wr_lo, br_col, *, tm, n_prompt_rows, alpha):
    r, d = x.shape
    d_a, d_b, d_c = ma.shape[1], mbp.shape[1], mc.shape[1]
    npt = n_prompt_rows // tm
    rows = lambda i: (i, 0)
    return pl.pallas_call(
        functools.partial(_mix_kernel, n_prompt_tiles=npt, alpha=alpha),
        grid=(r // tm,),
        in_specs=[
            pl.BlockSpec((tm, d_a), rows),
            pl.BlockSpec((tm, d_b), lambda i: (jnp.minimum(i, npt - 1), 0)),
            pl.BlockSpec((tm, d_b), lambda i: (jnp.maximum(i - npt, 0), 0)),
            pl.BlockSpec((tm, d_c), rows),
            pl.BlockSpec((tm, d), rows),
            _resident((d_a + d_b + d_c, d)),
            _resident((1, d)),
            _resident((1, d)),
            _resident((d, LANES)),
            _resident((d, LANES)),
            _resident((LANES, 1)),
        ],
        out_specs=[
            pl.BlockSpec((tm, d + LANES), rows),
            pl.BlockSpec((1, tm), lambda i: (0, i)),
        ],
        out_shape=[
            jax.ShapeDtypeStruct((r, d + LANES), F32),
            jax.ShapeDtypeStruct((1, r), jnp.int32),
        ],
        scratch_shapes=[pltpu.VMEM((tm, d_a + d_b + d_c), BF16)],
        compiler_params=_params(1, 40),
        name="mix",
    )(ma, mbp, mbs, mc, x, w_out, g1, b1, wr_hi, wr_lo, br_col)


def _moe_kernel(tok_ref, tg_ref, tr_ref, x1g_hbm, wg_ref, wu_ref, wd_ref, g2_ref, b2_ref, out_hbm,
                xg_sc, xb_sc, acc_sc, y_sc, sem, *, tm, d, alpha):
    t = pl.program_id(0)
    j = pl.program_id(1)
    n_real = tr_ref[t]
    valid = n_real > 0

    def row_in(r):
        return pltpu.make_async_copy(x1g_hbm.at[pl.ds(tok_ref[t * tm + r], 1), :], xg_sc.at[pl.ds(r, 1), :],
                                     sem.at[0])

    def row_out(r):
        return pltpu.make_async_copy(y_sc.at[pl.ds(r, 1), :], out_hbm.at[pl.ds(tok_ref[t * tm + r], 1), :],
                                     sem.at[1])

    def for_rows(n, fn):
        def body(r, carry):
            fn(r)
            return carry
        lax.fori_loop(0, n, body, 0)

    @pl.when(jnp.logical_and(valid, j == 0))
    def _():
        for_rows(tm, lambda r: row_in(r).start())
        for_rows(tm, lambda r: row_in(r).wait())
        xb_sc[...] = xg_sc[:, :d].astype(BF16)
        acc_sc[...] = jnp.zeros_like(acc_sc)

    @pl.when(valid)
    def _():
        xb = xb_sc[...]
        hid = (jax.nn.silu(jnp.dot(xb, wg_ref[...], preferred_element_type=F32))
               * jnp.dot(xb, wu_ref[...], preferred_element_type=F32))
        lane = lax.broadcasted_iota(jnp.int32, (tm, LANES), 1)
        expert = tg_ref[t] * EXPERTS_PER_GROUP + j
        gate = jnp.sum(jnp.where(lane == expert, xg_sc[:, d:], 0.0), -1, keepdims=True)
        acc_sc[...] += jnp.dot((hid * gate).astype(BF16), wd_ref[...], preferred_element_type=F32)

    @pl.when(jnp.logical_and(valid, j == EXPERTS_PER_GROUP - 1))
    def _():
        y_sc[...] = _layer_norm(alpha * xg_sc[:, :d] + acc_sc[...], g2_ref[...], b2_ref[...])
        for_rows(n_real, lambda r: row_out(r).start())
        for_rows(n_real, lambda r: row_out(r).wait())


def _moe(token, tile_group, tile_rows, x1g, wg, wu, wd, g2, b2, *, tm, n_rows, alpha):
    d = x1g.shape[1] - LANES
    d_ff = wg.shape[-1]
    n_tiles = tile_group.shape[0]

    def expert(t, j, tok, tg, tr):
        return tg[t] * EXPERTS_PER_GROUP + jnp.where(tr[t] > 0, j, EXPERTS_PER_GROUP - 1)

    grid_spec = pltpu.PrefetchScalarGridSpec(
        num_scalar_prefetch=3,
        grid=(n_tiles, EXPERTS_PER_GROUP),
        in_specs=[
            pl.BlockSpec(memory_space=pl.ANY),
            pl.BlockSpec((None, d, d_ff), lambda *a: (expert(*a), 0, 0)),
            pl.BlockSpec((None, d, d_ff), lambda *a: (expert(*a), 0, 0)),
            pl.BlockSpec((None, d_ff, d), lambda *a: (expert(*a), 0, 0)),
            pl.BlockSpec((1, d), lambda *a: (0, 0)),
            pl.BlockSpec((1, d), lambda *a: (0, 0)),
        ],
        out_specs=pl.BlockSpec(memory_space=pl.ANY),
        scratch_shapes=[
            pltpu.VMEM((tm, d + LANES), F32),
            pltpu.VMEM((tm, d), BF16),
            pltpu.VMEM((tm, d), F32),
            pltpu.VMEM((tm, d), F32),
            pltpu.SemaphoreType.DMA((2,)),
        ],
    )
    return pl.pallas_call(
        functools.partial(_moe_kernel, tm=tm, d=d, alpha=alpha),
        grid_spec=grid_spec,
        out_shape=jax.ShapeDtypeStruct((n_rows, d), F32),
        compiler_params=_params(2, 48),
        name="moe",
    )(token, tile_group, tile_rows, x1g, wg, wu, wd, g2, b2)


def _sort_by_group(gid, *, tm, n_tiles):
    r = gid.shape[0]
    onehot = (gid[:, None] == jnp.arange(N_GROUPS, dtype=jnp.int32)[None, :]).astype(jnp.int32)
    rank = jnp.sum((jnp.cumsum(onehot, axis=0) - onehot) * onehot, axis=1)
    counts = jnp.sum(onehot, axis=0)
    tiles = (counts + tm - 1) // tm
    tile_end = jnp.cumsum(tiles)
    tile_start = tile_end - tiles
    slot = tile_start[gid] * tm + rank
    token = jnp.zeros((n_tiles * tm,), jnp.int32).at[slot].set(jnp.arange(r, dtype=jnp.int32))
    tile_ids = jnp.arange(n_tiles, dtype=jnp.int32)
    tile_group = jnp.minimum(jnp.sum(tile_ids[:, None] >= tile_end[None, :], axis=1), N_GROUPS - 1).astype(jnp.int32)
    tile_rows = jnp.clip(counts[tile_group] - (tile_ids - tile_start[tile_group]) * tm, 0, tm)
    tile_rows = jnp.where(tile_ids < tile_end[-1], tile_rows, 0).astype(jnp.int32)
    last_group = tile_group[jnp.maximum(tile_end[-1] - 1, 0)]
    tile_group = jnp.where(tile_rows > 0, tile_group, last_group)
    return token, tile_group, tile_rows


def _embed_kernel(x_ref, p_ref, wg_ref, bg_ref, wp_ref, o_ref):
    x = x_ref[...]
    gate = jax.nn.sigmoid(jnp.dot(x.astype(BF16), wg_ref[...], preferred_element_type=F32) + bg_ref[...])
    proj = jnp.dot(p_ref[...].astype(BF16), wp_ref[...], preferred_element_type=F32)
    o_ref[...] = x + gate * proj


def _embed(x2, p, w_pg, b_pg, w_pp, *, tm, n_rows):
    d = x2.shape[1]
    d_p = p.shape[1]
    rows = lambda i: (i, 0)
    return pl.pallas_call(
        _embed_kernel,
        grid=(n_rows // tm,),
        in_specs=[
            pl.BlockSpec((tm, d), rows),
            pl.BlockSpec((tm, d_p), rows),
            _resident((d, d)),
            _resident((1, d)),
            _resident((d_p, d)),
        ],
        out_specs=pl.BlockSpec((tm, d), rows),
        out_shape=jax.ShapeDtypeStruct((n_rows, d), F32),
        compiler_params=_params(1, 40),
        name="embed",
    )(x2, p, w_pg, b_pg, w_pp)


def kernel(x_prompt, x_sample, p_prompt, p_sample, cache_k, cache_v, cache_logf, state_conv, page_table, w_in, b_f, g_v, w_s, b_s, conv_w, g_mix, w_out, ln1_g, ln1_b, w_router, b_router, w_gate, w_up, w_down, ln2_g, ln2_b, w_ple_gate, b_ple_gate, w_ple_proj):
    depth = w_in.shape[0]
    batch, seq, d = x_prompt.shape
    n_seq, dec_seq, _ = x_sample.shape
    n_heads = cache_k.shape[3]
    d_a = g_v.shape[-1]
    d_b = n_heads * HEAD_W
    d_c = conv_w.shape[-1]
    np_rows = batch * seq
    ns_rows = n_seq * dec_seq
    n_rows = np_rows + ns_rows
    alpha = (2 * depth) ** 0.25
    tm = 256
    tm_moe = 512
    tq = 512
    assert batch == 1 and CONV_W - 1 <= dec_seq <= CHUNK and dec_seq & (dec_seq - 1) == 0
    assert np_rows % tq == 0 and ns_rows % tm == 0 and np_rows % tm_moe == 0 and ns_rows % tm_moe == 0
    assert w_s.shape[-1] == CHUNK and cache_k.shape[-1] == HEAD_W and n_heads <= SUBLANES
    assert w_gate.shape[1] == N_EXPERTS and w_in.shape[-1] == 2 * d_a + 3 * d_b + n_heads + 3 * d_c
    n_tiles = n_rows // tm_moe + N_GROUPS

    x = jnp.concatenate([x_prompt.reshape(np_rows, d), x_sample.reshape(ns_rows, d)], axis=0)
    wr_pad = jnp.pad(w_router, ((0, 0), (0, LANES - N_EXPERTS)))
    wr_hi = wr_pad.astype(BF16)
    wr_lo = (wr_pad - wr_hi.astype(F32)).astype(BF16)
    br_col = jnp.pad(b_router, (0, LANES - N_EXPERTS)).reshape(LANES, 1)
    reps = CHUNK // dec_seq

    outs = [[] for _ in range(9)]
    for i in range(depth):
        o_q, o_k, o_v = 2 * d_a, 2 * d_a + d_b, 2 * d_a + 2 * d_b
        o_f = 2 * d_a + 3 * d_b
        o_c = o_f + n_heads
        wi = w_in[i]
        w_a = wi[:, :o_q].astype(BF16)
        w_q = wi[:, o_q:o_k].astype(BF16)
        w_k = wi[:, o_k:o_v].astype(BF16)
        w_v = wi[:, o_v:o_f].astype(BF16)
        w_c = jnp.concatenate([wi[:, o_c:], jnp.pad(wi[:, o_f:o_c], ((0, 0), (0, LANES - n_heads)))], axis=1).astype(BF16)
        bf_pad = jnp.pad(b_f[i], (0, LANES - n_heads)).reshape(1, LANES)
        gm = g_mix[i].reshape(1, -1)
        gm_a, gm_b, gm_c = gm[:, :d_a], gm[:, d_a:d_a + d_b], gm[:, d_a + d_b:]
        ws_all = jnp.stack([w_s[i], jnp.tile(w_s[i][:, :dec_seq, :dec_seq], (1, reps, reps))])
        bs_rows = lambda b: jnp.repeat(b.T, HEAD_W, axis=1)
        bs_all = jnp.stack([bs_rows(b_s[i]), bs_rows(jnp.tile(b_s[i][:, :dec_seq], (1, reps)))])
        st = state_conv[i]
        zero_rows = jnp.zeros((n_seq, dec_seq - 1, d_c), F32)
        s1 = jnp.concatenate([st[:, 1:2], zero_rows], axis=1).reshape(ns_rows, d_c)
        s2 = jnp.concatenate([st[:, 0:1], st[:, 1:2], zero_rows[:, 1:]], axis=1).reshape(ns_rows, d_c)

        av_s, ma = _branch_a(x, w_a, g_v[i].reshape(1, d_a), ws_all, bs_all, gm_a,
                             tm=tm, n_prompt_rows=np_rows, dec_seq=dec_seq)
        qb, qs = _project(x, w_q, tm=tm, flat_dtypes=[BF16], sample_f32_rows=(np_rows // tm, ns_rows))
        kb, kf = _project(x, w_k, tm=tm, flat_dtypes=[BF16], heads_f32=True)
        vb, vf = _project(x, w_v, tm=tm, flat_dtypes=[BF16], heads_f32=True)
        zc, mc, lf, cs = _branch_c(x, w_c, bf_pad, conv_w[i], gm_c, s1, s2,
                                   tm=tm, n_prompt_rows=np_rows, dec_seq=dec_seq)

        cneg = (-cs[:np_rows, :n_heads]).T.reshape(n_heads, 1, np_rows)
        mbp = _prompt_attention(qb, kb, vb, cneg, gm_b, n_prompt_rows=np_rows, tq=tq)

        lf_new = lf[np_rows:, :n_heads].reshape(n_seq, dec_seq, n_heads)
        lf_past = cache_logf[i][page_table].reshape(n_seq, -1, n_heads)
        lf_all = jnp.concatenate([lf_past, lf_new], axis=1)
        suffix = jnp.flip(jnp.cumsum(jnp.flip(lf_all, 1), axis=1), 1) - lf_all
        past = lf_past.shape[1]
        mbs = _sample_attention(page_table, qs, kf, vf, suffix[:, :past], suffix[:, past:], gm_b, cache_k, cache_v,
                                layer=i, n_prompt_rows=np_rows, dec_seq=dec_seq)

        x1g, gid = _mix(ma, mbp, mbs, mc, x, w_out[i].astype(BF16), ln1_g[i].reshape(1, d), ln1_b[i].reshape(1, d),
                        wr_hi, wr_lo, br_col, tm=tm, n_prompt_rows=np_rows, alpha=alpha)
        token, tile_group, tile_rows = _sort_by_group(gid[0], tm=tm_moe, n_tiles=n_tiles)
        x2 = _moe(token, tile_group, tile_rows, x1g, w_gate[i].astype(BF16), w_up[i].astype(BF16),
                  w_down[i].astype(BF16), ln2_g[i].reshape(1, d), ln2_b[i].reshape(1, d),
                  tm=tm_moe, n_rows=n_rows, alpha=alpha)
        p = jnp.concatenate([p_prompt[i].reshape(np_rows, -1), p_sample[i].reshape(ns_rows, -1)], axis=0)
        x = _embed(x2, p, w_ple_gate[i].astype(BF16), b_ple_gate[i].reshape(1, d), w_ple_proj[i].astype(BF16),
                   tm=tm, n_rows=n_rows)

        heads = lambda a, lead: a.reshape(*lead, n_heads, HEAD_W)
        outs[0].append(heads(kf[:np_rows], (batch, seq)))
        outs[1].append(heads(vf[:np_rows], (batch, seq)))
        outs[2].append(lf[:np_rows, :n_heads].reshape(batch, seq, n_heads))
        outs[3].append(zc[np_rows - (CONV_W - 1):np_rows].reshape(batch, CONV_W - 1, d_c))
        outs[4].append(av_s.reshape(n_seq, dec_seq, d_a))
        outs[5].append(heads(kf[np_rows:], (n_seq, dec_seq)))
        outs[6].append(heads(vf[np_rows:], (n_seq, dec_seq)))
        outs[7].append(lf_new)
        outs[8].append(zc[np_rows:].reshape(n_seq, dec_seq, d_c)[:, dec_seq - (CONV_W - 1):])

    y_prompt = x[:np_rows].reshape(batch, seq, d)
    y_sample = x[np_rows:].reshape(n_seq, dec_seq, d)
    return (y_prompt, y_sample) + tuple(jnp.stack(o) for o in outs)
```

```python
import functools

import numpy as np
import jax
import jax.numpy as jnp
from jax import lax
from jax.experimental import pallas as pl
from jax.experimental.pallas import tpu as pltpu

F32 = jnp.float32
BF16 = jnp.bfloat16

HEAD_W = 128
CHUNK = 128
CONV_W = 3
N_GROUPS = 4
EXPERTS_PER_GROUP = 4
N_EXPERTS = N_GROUPS * EXPERTS_PER_GROUP
LN_EPS = 1e-5
RMS_EPS = 1e-6
ATTN_SCALE = HEAD_W ** -0.5
LOG2E = float(np.log2(np.e))
LANES = 128
SUBLANES = 8
MIB = 1024 * 1024


def _params(n_axes, vmem_mib):
    return pltpu.CompilerParams(dimension_semantics=("arbitrary",) * n_axes,
                                vmem_limit_bytes=vmem_mib * MIB)


def _resident(shape):
    zeros = (0,) * len(shape)
    return pl.BlockSpec(shape, lambda *_: zeros, pipeline_mode=pl.Buffered(1))


def _layer_norm(v, gain, bias=None):
    mu = jnp.mean(v, -1, keepdims=True)
    var = jnp.mean(jnp.square(v - mu), -1, keepdims=True)
    y = (v - mu) * lax.rsqrt(var + LN_EPS) * gain
    return y if bias is None else y + bias


def _head_rms(o):
    return o * lax.rsqrt(jnp.mean(o * o, -1, keepdims=True) + RMS_EPS)


def _split3(v):
    hi = v.astype(BF16)
    r1 = v - hi.astype(F32)
    mid = r1.astype(BF16)
    lo = (r1 - mid.astype(F32)).astype(BF16)
    return hi, mid, lo


def _proj_kernel(*refs, n_prompt_tiles, flat, t_scale, sample_rows, cache):
    refs = iter(refs)
    x_ref, w_ref = next(refs), next(refs)
    if cache:
        next(refs), next(refs)
    i = pl.program_id(0)
    z = jnp.dot(x_ref[...].astype(BF16), w_ref[...], preferred_element_type=F32)
    if flat:
        next(refs)[...] = z.astype(BF16)
    if t_scale is not None:
        next(refs)[...] = (z * t_scale).T.astype(BF16)
    if sample_rows:
        next(refs)[...] = z
    if cache:
        prompt_ref, sample_ref = next(refs), next(refs)

        def heads_to(o_ref):
            for h in range(o_ref.shape[1]):
                o_ref[:, h, :] = z[:, h * HEAD_W:(h + 1) * HEAD_W]

        @pl.when(i < n_prompt_tiles)
        def _():
            heads_to(prompt_ref)

        @pl.when(i >= n_prompt_tiles)
        def _():
            heads_to(sample_ref)


def _project(x, w, *, tm, n_prompt_rows, flat=False, t_scale=None, sample_rows=False, cache=None):
    r, d = x.shape
    n = w.shape[1]
    npt = n_prompt_rows // tm
    srows = lambda i: (jnp.maximum(i - npt, 0), 0)
    operands = [x, w]
    in_specs = [pl.BlockSpec((tm, d), lambda i: (i, 0)), _resident((d, n))]
    out_shape, out_specs, aliases = [], [], {}
    if flat:
        out_shape.append(jax.ShapeDtypeStruct((r, n), BF16))
        out_specs.append(pl.BlockSpec((tm, n), lambda i: (i, 0)))
    if t_scale is not None:
        out_shape.append(jax.ShapeDtypeStruct((r // tm, n, tm), BF16))
        out_specs.append(pl.BlockSpec((None, n, tm), lambda i: (i, 0, 0)))
    if sample_rows:
        out_shape.append(jax.ShapeDtypeStruct((r - n_prompt_rows, n), F32))
        out_specs.append(pl.BlockSpec((tm, n), srows))
    if cache is not None:
        prompt_buf, sample_buf, layer = cache
        heads = n // HEAD_W
        for buf, tile in ((prompt_buf, lambda i: jnp.minimum(i, npt - 1)), (sample_buf, lambda i: jnp.maximum(i - npt, 0))):
            aliases[len(operands)] = len(out_shape)
            operands.append(buf)
            in_specs.append(pl.BlockSpec(memory_space=pl.ANY))
            out_shape.append(jax.ShapeDtypeStruct(buf.shape, buf.dtype))
            out_specs.append(pl.BlockSpec((None, tm, heads, HEAD_W), functools.partial(
                lambda i, tile: (layer, tile(i), 0, 0), tile=tile)))
    kern = functools.partial(_proj_kernel, n_prompt_tiles=npt, flat=flat, t_scale=t_scale,
                             sample_rows=sample_rows, cache=cache is not None)
    return pl.pallas_call(
        kern,
        grid=(r // tm,),
        in_specs=in_specs,
        out_specs=out_specs,
        out_shape=out_shape,
        input_output_aliases=aliases,
        compiler_params=_params(1, 40),
        name="proj",
    )(*operands)


def _branch_a_kernel(x_ref, w_ref, gv_ref, ws_ref, bs_ref, gm_ref, av_ref, ma_ref, *,
                     n_prompt_tiles, tm, dec_shift):
    is_sample = pl.program_id(0) >= n_prompt_tiles
    d_a = gv_ref.shape[-1]
    z = jnp.dot(x_ref[...].astype(BF16), w_ref[...], preferred_element_type=F32)
    u = jax.nn.gelu(z[:, :d_a])
    av = _layer_norm(jax.nn.gelu(z[:, d_a:]), gv_ref[...])
    av_ref[...] = av
    avb = av.astype(BF16)
    row = lax.broadcasted_iota(jnp.int32, (CHUNK, CHUNK), 0)
    col = lax.broadcasted_iota(jnp.int32, (CHUNK, CHUNK), 1)
    shift = jnp.where(is_sample, dec_shift, int(np.log2(CHUNK)))
    mask = jnp.logical_and(col <= row, jnp.right_shift(row, shift) == jnp.right_shift(col, shift))
    for h in range(d_a // HEAD_W):
        hs = slice(h * HEAD_W, (h + 1) * HEAD_W)
        wm = jnp.where(mask, ws_ref[h], 0.0).astype(BF16)
        for c in range(tm // CHUNK):
            cs = slice(c * CHUNK, (c + 1) * CHUNK)
            mix = jnp.dot(wm, avb[cs, hs], preferred_element_type=F32) + bs_ref[:, hs]
            o = _head_rms(u[cs, hs] * mix) * gm_ref[:, hs]
            ma_ref[cs, hs] = o.astype(BF16)


def _branch_a(x, w_a, g_v, ws_all, bs_all, gm_a, *, tm, n_prompt_rows, dec_seq):
    r, d = x.shape
    d_a = g_v.shape[-1]
    npt = n_prompt_rows // tm
    n_heads = d_a // HEAD_W
    kern = functools.partial(_branch_a_kernel, n_prompt_tiles=npt, tm=tm, dec_shift=int(np.log2(dec_seq)))
    sel = lambda i: (i >= npt).astype(jnp.int32)
    return pl.pallas_call(
        kern,
        grid=(r // tm,),
        in_specs=[
            pl.BlockSpec((tm, d), lambda i: (i, 0)),
            _resident((d, 2 * d_a)),
            _resident((1, d_a)),
            pl.BlockSpec((None, n_heads, CHUNK, CHUNK), lambda i: (sel(i), 0, 0, 0)),
            pl.BlockSpec((None, CHUNK, d_a), lambda i: (sel(i), 0, 0)),
            _resident((1, d_a)),
        ],
        out_specs=[
            pl.BlockSpec((tm, d_a), lambda i: (jnp.maximum(i - npt, 0), 0)),
            pl.BlockSpec((tm, d_a), lambda i: (i, 0)),
        ],
        out_shape=[
            jax.ShapeDtypeStruct((r - n_prompt_rows, d_a), F32),
            jax.ShapeDtypeStruct((r, d_a), BF16),
        ],
        compiler_params=_params(1, 32),
        name="branch_a",
    )(x, w_a, g_v, ws_all, bs_all, gm_a)


def _branch_c_kernel(x_ref, w_ref, bf_ref, cw_ref, gm_ref, s1_ref, s2_ref,
                     zc_ref, mc_ref, lf_ref, kb_ref, carry_z, carry_c, *,
                     n_prompt_tiles, tm, dec_seq):
    i = pl.program_id(0)
    d_c = gm_ref.shape[-1]

    @pl.when(i == 0)
    def _():
        carry_z[...] = jnp.zeros_like(carry_z)
        carry_c[...] = jnp.zeros_like(carry_c)

    z = jnp.dot(x_ref[...].astype(BF16), w_ref[...], preferred_element_type=F32)
    c_b = z[:, :d_c]
    zc = z[:, d_c:2 * d_c] * z[:, 2 * d_c:3 * d_c]
    zc_ref[...] = zc

    lf = jax.nn.log_sigmoid(z[:, 3 * d_c:] + bf_ref[...])
    lf_ref[...] = lf
    trow = lax.broadcasted_iota(jnp.int32, (tm, tm), 0)
    tcol = lax.broadcasted_iota(jnp.int32, (tm, tm), 1)
    tri = jnp.where(tcol <= trow, 1.0, 0.0).astype(BF16)
    hi, mid, lo = _split3(lf)
    cs = (jnp.dot(tri, hi, preferred_element_type=F32) + jnp.dot(tri, mid, preferred_element_type=F32)
          + jnp.dot(tri, lo, preferred_element_type=F32)) + carry_c[...]
    carry_c[...] = cs[tm - 1:tm, :]
    neg = cs * (-LOG2E)
    lane = lax.broadcasted_iota(jnp.int32, (tm, HEAD_W), 1)
    for h in range(kb_ref.shape[-1] // HEAD_W):
        hi, mid, lo = (t.astype(F32) for t in _split3(jnp.broadcast_to(neg[:, h:h + 1], (tm, HEAD_W))))
        terms = jnp.where(lane == 0, hi, jnp.where(lane == 1, mid, jnp.where(lane == 2, lo, 0.0)))
        kb_ref[:, h * HEAD_W:(h + 1) * HEAD_W] = terms.astype(BF16)

    row = lax.broadcasted_iota(jnp.int32, (tm, d_c), 0)
    back1 = pltpu.roll(zc, 1, 0)
    back2 = pltpu.roll(zc, 2, 0)

    def finish(zm1, zm2):
        y = cw_ref[0:1, :] * zm2 + cw_ref[1:2, :] * zm1 + cw_ref[2:3, :] * zc
        o = c_b * y
        for g in range(d_c // HEAD_W):
            gs = slice(g * HEAD_W, (g + 1) * HEAD_W)
            mc_ref[:, gs] = (_head_rms(o[:, gs]) * gm_ref[:, gs]).astype(BF16)

    @pl.when(i < n_prompt_tiles)
    def _():
        p6 = carry_z[SUBLANES - 2:SUBLANES - 1, :]
        p7 = carry_z[SUBLANES - 1:SUBLANES, :]
        finish(jnp.where(row >= 1, back1, p7),
               jnp.where(row >= 2, back2, jnp.where(row == 0, p6, p7)))

    @pl.when(i >= n_prompt_tiles)
    def _():
        t = jnp.bitwise_and(row, dec_seq - 1)
        finish(jnp.where(t >= 1, back1, s1_ref[...]), jnp.where(t >= 2, back2, s2_ref[...]))

    carry_z[...] = zc[tm - SUBLANES:tm, :]


def _branch_c(x, w_c, bf_pad, conv_w, gm_c, s1, s2, *, tm, n_prompt_rows, dec_seq, d_b):
    r, d = x.shape
    d_c = gm_c.shape[-1]
    n = w_c.shape[1]
    npt = n_prompt_rows // tm
    kern = functools.partial(_branch_c_kernel, n_prompt_tiles=npt, tm=tm, dec_seq=dec_seq)
    rows = lambda i: (i, 0)
    srows = lambda i: (jnp.maximum(i - npt, 0), 0)
    return pl.pallas_call(
        kern,
        grid=(r // tm,),
        in_specs=[
            pl.BlockSpec((tm, d), rows),
            _resident((d, n)),
            _resident((1, LANES)),
            _resident((CONV_W, d_c)),
            _resident((1, d_c)),
            pl.BlockSpec((tm, d_c), srows),
            pl.BlockSpec((tm, d_c), srows),
        ],
        out_specs=[
            pl.BlockSpec((tm, d_c), rows),
            pl.BlockSpec((tm, d_c), rows),
            pl.BlockSpec((tm, LANES), rows),
            pl.BlockSpec((tm, d_b), rows),
        ],
        out_shape=[
            jax.ShapeDtypeStruct((r, d_c), F32),
            jax.ShapeDtypeStruct((r, d_c), BF16),
            jax.ShapeDtypeStruct((r, LANES), F32),
            jax.ShapeDtypeStruct((r, d_b), BF16),
        ],
        scratch_shapes=[pltpu.VMEM((SUBLANES, d_c), F32), pltpu.VMEM((1, LANES), F32)],
        compiler_params=_params(1, 40),
        name="branch_c",
    )(x, w_c, bf_pad, conv_w, gm_c, s1, s2)


def _prompt_attn_kernel(qt_ref, k_ref, kb_ref, vt_ref, gm_ref, o_ref, m_sc, l_sc, acc_sc, *, tq):
    qi = pl.program_id(1)
    sub = lax.broadcasted_iota(jnp.int32, (HEAD_W, tq), 0)
    q_aug = jnp.concatenate([qt_ref[...], jnp.where(sub < 3, 1.0, 0.0).astype(BF16)], axis=0)
    m_sc[...] = jnp.full_like(m_sc, -jnp.inf)
    l_sc[...] = jnp.zeros_like(l_sc)
    acc_sc[...] = jnp.zeros_like(acc_sc)

    def block(j, diagonal):
        k_aug = jnp.concatenate([k_ref[j], kb_ref[j]], axis=1)
        st = jnp.dot(k_aug, q_aug, preferred_element_type=F32)
        if diagonal:
            key = lax.broadcasted_iota(jnp.int32, (tq, tq), 0)
            query = lax.broadcasted_iota(jnp.int32, (tq, tq), 1)
            st = jnp.where(key <= query, st, -jnp.inf)
        m_old = m_sc[...]
        m_new = jnp.maximum(m_old, jnp.max(st, axis=0, keepdims=True))
        alpha = jnp.exp2(m_old - m_new)
        pt = jnp.exp2(st - m_new)
        l_sc[...] = alpha * l_sc[...] + jnp.sum(pt, axis=0, keepdims=True)
        acc_sc[...] = alpha * acc_sc[...] + jnp.dot(vt_ref[j], pt.astype(BF16), preferred_element_type=F32)
        m_sc[...] = m_new

    def body(j, carry):
        block(j, False)
        return carry

    lax.fori_loop(0, qi, body, 0)
    block(qi, True)
    o = (acc_sc[...] / l_sc[...]).T
    o_ref[...] = (_head_rms(o) * gm_ref[...]).astype(o_ref.dtype)


def _prompt_attention(qt, kb, kbias, vt, gm_b, *, n_prompt_rows, tq):
    d_b = kb.shape[1]
    n_heads = d_b // HEAD_W
    nq = n_prompt_rows // tq
    kb3 = kb.reshape(-1, tq, d_b)
    kbias3 = kbias.reshape(-1, tq, d_b)
    key_rows = pl.BlockSpec((nq, tq, HEAD_W), lambda h, qi: (0, 0, h))
    return pl.pallas_call(
        functools.partial(_prompt_attn_kernel, tq=tq),
        grid=(n_heads, nq),
        in_specs=[
            pl.BlockSpec((None, HEAD_W, tq), lambda h, qi: (qi, h, 0)),
            key_rows,
            key_rows,
            pl.BlockSpec((nq, HEAD_W, tq), lambda h, qi: (0, h, 0)),
            pl.BlockSpec((1, HEAD_W), lambda h, qi: (0, h)),
        ],
        out_specs=pl.BlockSpec((tq, HEAD_W), lambda h, qi: (qi, h)),
        out_shape=jax.ShapeDtypeStruct((n_prompt_rows, d_b), BF16),
        scratch_shapes=[pltpu.VMEM((1, tq), F32), pltpu.VMEM((1, tq), F32), pltpu.VMEM((HEAD_W, tq), F32)],
        compiler_params=_params(2, 32),
        name="prompt_attn",
    )(qt, kb3, kbias3, vt, gm_b)


def _sample_attn_kernel(pt_ref, q_ref, kn_ref, vn_ref, bp_ref, bn_ref, gm_ref, ck_hbm, cv_hbm, o_ref,
                        kbuf, vbuf, kb_sc, vb_sc, s_sc, p_sc, sem, *, layer, n_pages, page, dec_seq):
    n = pl.program_id(0)
    d_b = q_ref.shape[-1]
    n_heads = d_b // HEAD_W
    past = n_pages * page
    n_keys = past + LANES

    def page_copies(seq, slot):
        copies = []
        for p in range(n_pages):
            pg = pt_ref[seq, p]
            rows = pl.ds(p * page * n_heads, page * n_heads)
            copies.append(pltpu.make_async_copy(ck_hbm.at[layer, pg], kbuf.at[slot, rows], sem.at[0, slot]))
            copies.append(pltpu.make_async_copy(cv_hbm.at[layer, pg], vbuf.at[slot, rows], sem.at[1, slot]))
        return copies

    @pl.when(n == 0)
    def _():
        for c in page_copies(0, 0):
            c.start()

    @pl.when(n + 1 < pl.num_programs(0))
    def _():
        for c in page_copies(n + 1, (n + 1) % 2):
            c.start()

    slot = n % 2
    for c in page_copies(n, slot):
        c.wait()

    qrep = jnp.concatenate([q_ref[...]] * (LANES // dec_seq), axis=0)
    rowi = lax.broadcasted_iota(jnp.int32, (LANES, d_b), 0)
    lanei = lax.broadcasted_iota(jnp.int32, (LANES, d_b), 1)
    qbd = jnp.where(lanei // HEAD_W == rowi // dec_seq, qrep, 0.0).astype(BF16)

    pad = jnp.zeros((LANES - dec_seq, HEAD_W), F32)
    for h in range(n_heads):
        hs = slice(h * HEAD_W, (h + 1) * HEAD_W)
        head_rows = pl.ds(h, past, stride=n_heads)
        kb_sc[:past, hs] = kbuf.at[slot][head_rows, :].astype(BF16)
        vb_sc[:past, hs] = vbuf.at[slot][head_rows, :].astype(BF16)
        kb_sc[past:, hs] = jnp.concatenate([kn_ref[:, h, :], pad], axis=0).astype(BF16)
        vb_sc[past:, hs] = jnp.concatenate([vn_ref[:, h, :], pad], axis=0).astype(BF16)

    sc = lax.dot_general(kb_sc[...], qbd, (((1,), (1,)), ((), ())), preferred_element_type=F32) * ATTN_SCALE

    def expand(b):
        col = lax.broadcasted_iota(jnp.int32, (b.shape[0], LANES), 1)
        out = jnp.zeros((b.shape[0], LANES), F32)
        for h in range(n_heads):
            out = jnp.where(col // dec_seq == h, b[:, h:h + 1], out)
        return out

    bp = bp_ref[...]
    pieces = [jnp.broadcast_to(bp[h:h + 1, :], (dec_seq, past)) for h in range(n_heads)]
    if n_heads * dec_seq < LANES:
        pieces.append(jnp.zeros((LANES - n_heads * dec_seq, past), F32))
    b_rep = jnp.concatenate(pieces, axis=0)
    for c in range(past // LANES):
        cs = slice(c * LANES, (c + 1) * LANES)
        s_sc[cs, :] = sc[cs, :] + b_rep[:, cs].T
    krow = lax.broadcasted_iota(jnp.int32, (LANES, LANES), 0)
    qcol = lax.broadcasted_iota(jnp.int32, (LANES, LANES), 1) % dec_seq
    bn = jnp.concatenate([expand(bn_ref[...]), jnp.zeros((LANES - dec_seq, LANES), F32)], axis=0)
    s_sc[past:, :] = jnp.where(krow <= qcol, sc[past:, :] + bn, -jnp.inf)

    m = jnp.max(s_sc[...], axis=0, keepdims=True)
    l = jnp.zeros((LANES, 1), F32)
    for c in range(n_keys // LANES):
        cs = slice(c * LANES, (c + 1) * LANES)
        pt = jnp.exp(s_sc[cs, :] - m).T
        l = l + jnp.sum(pt, -1, keepdims=True)
        p_sc[:, cs] = pt.astype(BF16)
    acc = jnp.dot(p_sc[...], vb_sc[...], preferred_element_type=F32)
    for h in range(n_heads):
        rs = slice(h * dec_seq, (h + 1) * dec_seq)
        hs = slice(h * HEAD_W, (h + 1) * HEAD_W)
        o = acc[rs, hs] / l[rs, :]
        o_ref[:, hs] = _head_rms(o) * gm_ref[:, hs]


def _sample_attention(page_table, qs, k_new, v_new, bias_past, bias_new, gm_b, cache_k, cache_v, *,
                      layer, dec_seq):
    n_seq, n_pages = page_table.shape
    depth, n_pool, page, n_heads, head_w = cache_k.shape
    d_b = n_heads * head_w
    past = n_pages * page
    cache_k = cache_k.reshape(depth, n_pool, page * n_heads, head_w)
    cache_v = cache_v.reshape(depth, n_pool, page * n_heads, head_w)

    new_rows = pl.BlockSpec((None, dec_seq, n_heads, head_w), lambda n, pt: (layer, n, 0, 0))
    grid_spec = pltpu.PrefetchScalarGridSpec(
        num_scalar_prefetch=1,
        grid=(n_seq,),
        in_specs=[
            pl.BlockSpec((dec_seq, d_b), lambda n, pt: (n, 0)),
            new_rows,
            new_rows,
            pl.BlockSpec((None, n_heads, past), lambda n, pt: (n, 0, 0)),
            pl.BlockSpec((None, dec_seq, n_heads), lambda n, pt: (n, 0, 0)),
            pl.BlockSpec((1, d_b), lambda n, pt: (0, 0)),
            pl.BlockSpec(memory_space=pl.ANY),
            pl.BlockSpec(memory_space=pl.ANY),
        ],
        out_specs=pl.BlockSpec((dec_seq, d_b), lambda n, pt: (n, 0)),
        scratch_shapes=[
            pltpu.VMEM((2, past * n_heads, head_w), F32),
            pltpu.VMEM((2, past * n_heads, head_w), F32),
            pltpu.VMEM((past + LANES, d_b), BF16),
            pltpu.VMEM((past + LANES, d_b), BF16),
            pltpu.VMEM((past + LANES, LANES), F32),
            pltpu.VMEM((LANES, past + LANES), BF16),
            pltpu.SemaphoreType.DMA((2, 2)),
        ],
    )
    kern = functools.partial(_sample_attn_kernel, layer=layer, n_pages=n_pages, page=page, dec_seq=dec_seq)
    return pl.pallas_call(
        kern,
        grid_spec=grid_spec,
        out_shape=jax.ShapeDtypeStruct((n_seq * dec_seq, d_b), F32),
        compiler_params=_params(1, 56),
        name="sample_attn",
    )(page_table, qs, k_new, v_new, bias_past, bias_new, gm_b, cache_k, cache_v)


def _mix_kernel(ma_ref, mbp_ref, mbs_ref, mc_ref, x_ref, w_ref, g1_ref, b1_ref, wrh_ref, wrl_ref, br_ref,
                x1g_ref, gid_ref, mixed_sc, *, n_prompt_tiles, alpha):
    i = pl.program_id(0)
    tm, d = x_ref.shape
    d_a = ma_ref.shape[-1]
    d_b = mbp_ref.shape[-1]
    mixed_sc[:, :d_a] = ma_ref[...]
    mixed_sc[:, d_a + d_b:] = mc_ref[...]

    @pl.when(i < n_prompt_tiles)
    def _():
        mixed_sc[:, d_a:d_a + d_b] = mbp_ref[...]

    @pl.when(i >= n_prompt_tiles)
    def _():
        mixed_sc[:, d_a:d_a + d_b] = mbs_ref[...].astype(BF16)

    y = jnp.dot(mixed_sc[...], w_ref[...], preferred_element_type=F32)
    x1 = _layer_norm(alpha * x_ref[...] + y, g1_ref[...], b1_ref[...])
    x1g_ref[:, :d] = x1

    hi = x1.astype(BF16)
    lo = (x1 - hi.astype(F32)).astype(BF16)
    logits = (jnp.dot(hi, wrh_ref[...], preferred_element_type=F32)
              + jnp.dot(lo, wrh_ref[...], preferred_element_type=F32)
              + jnp.dot(hi, wrl_ref[...], preferred_element_type=F32))
    score_all = jax.nn.sigmoid(logits.T)
    sel_all = score_all + br_ref[...]
    score = [score_all[e:e + 1, :] for e in range(N_EXPERTS)]
    sel = [sel_all[e:e + 1, :] for e in range(N_EXPERTS)]

    gscore = []
    for g in range(N_GROUPS):
        a0, a1, a2, a3 = sel[4 * g:4 * g + 4]
        h1, l1 = jnp.maximum(a0, a1), jnp.minimum(a0, a1)
        h2, l2 = jnp.maximum(a2, a3), jnp.minimum(a2, a3)
        gscore.append(jnp.maximum(h1, h2) + jnp.maximum(jnp.minimum(h1, h2), jnp.maximum(l1, l2)))
    gmax = functools.reduce(jnp.maximum, gscore)
    gid = jnp.full(gscore[0].shape, N_GROUPS - 1, jnp.int32)
    for g in range(N_GROUPS - 2, -1, -1):
        gid = jnp.where(gscore[g] == gmax, g, gid)
    gid_ref[...] = gid

    chosen = []
    for e in range(N_EXPERTS):
        g = e // EXPERTS_PER_GROUP
        rank = jnp.zeros(sel[e].shape, jnp.int32)
        for o in range(4 * g, 4 * g + 4):
            if o < e:
                rank = rank + (sel[o] >= sel[e]).astype(jnp.int32)
            elif o > e:
                rank = rank + (sel[o] > sel[e]).astype(jnp.int32)
        chosen.append(jnp.logical_and(gid == g, rank < 2))
    wsum = functools.reduce(jnp.add, [jnp.where(chosen[e], score[e], 0.0) for e in range(N_EXPERTS)])
    rowi = lax.broadcasted_iota(jnp.int32, (N_EXPERTS, tm), 0)
    gates_t = jnp.zeros((N_EXPERTS, tm), F32)
    for e in range(N_EXPERTS):
        gates_t = jnp.where(rowi == e, jnp.where(chosen[e], score[e] / wsum, 0.0), gates_t)
    gates_t = jnp.concatenate([gates_t, jnp.zeros((LANES - N_EXPERTS, tm), F32)], axis=0)
    x1g_ref[:, d:] = gates_t.T


def _mix(ma, mbp, mbs, mc, x, w_out, g1, b1, wr_hi, wr_lo, br_col, *, tm, n_prompt_rows, alpha):
    r, d = x.shape
    d_a, d_b, d_c = ma.shape[1], mbp.shape[1], mc.shape[1]
    npt = n_prompt_rows // tm
    rows = lambda i: (i, 0)
    return pl.pallas_call(
        functools.partial(_mix_kernel, n_prompt_tiles=npt, alpha=alpha),
        grid=(r // tm,),
        in_specs=[
            pl.BlockSpec((tm, d_a), rows),
            pl.BlockSpec((tm, d_b), lambda i: (jnp.minimum(i, npt - 1), 0)),
            pl.BlockSpec((tm, d_b), lambda i: (jnp.maximum(i - npt, 0), 0)),
            pl.BlockSpec((tm, d_c), rows),
            pl.BlockSpec((tm, d), rows),
            _resident((d_a + d_b + d_c, d)),
            _resident((1, d)),
            _resident((1, d)),
            _resident((d, LANES)),
            _resident((d, LANES)),
            _resident((LANES, 1)),
        ],
        out_specs=[
            pl.BlockSpec((tm, d + LANES), rows),
            pl.BlockSpec((1, tm), lambda i: (0, i)),
        ],
        out_shape=[
            jax.ShapeDtypeStruct((r, d + LANES), F32),
            jax.ShapeDtypeStruct((1, r), jnp.int32),
        ],
        scratch_shapes=[pltpu.VMEM((tm, d_a + d_b + d_c), BF16)],
        compiler_params=_params(1, 40),
        name="mix",
    )(ma, mbp, mbs, mc, x, w_out, g1, b1, wr_hi, wr_lo, br_col)


def _moe_kernel(tok_ref, tg_ref, tr_ref, x1g_hbm, wg_ref, wu_ref, wd_ref, g2_ref, b2_ref, out_hbm,
                xg_sc, xb_sc, acc_sc, y_sc, sem, *, tm, d, alpha):
    t = pl.program_id(0)
    j = pl.program_id(1)
    n_real = tr_ref[t]
    valid = n_real > 0

    def row_in(r):
        return pltpu.make_async_copy(x1g_hbm.at[pl.ds(tok_ref[t * tm + r], 1), :], xg_sc.at[pl.ds(r, 1), :],
                                     sem.at[0])

    def row_out(r):
        return pltpu.make_async_copy(y_sc.at[pl.ds(r, 1), :], out_hbm.at[pl.ds(tok_ref[t * tm + r], 1), :],
                                     sem.at[1])

    def for_rows(n, fn):
        def body(r, carry):
            fn(r)
            return carry
        lax.fori_loop(0, n, body, 0)

    @pl.when(jnp.logical_and(valid, j == 0))
    def _():
        for_rows(tm, lambda r: row_in(r).start())
        for_rows(tm, lambda r: row_in(r).wait())
        xb_sc[...] = xg_sc[:, :d].astype(BF16)
        acc_sc[...] = jnp.zeros_like(acc_sc)

    @pl.when(valid)
    def _():
        xb = xb_sc[...]
        hid = (jax.nn.silu(jnp.dot(xb, wg_ref[...], preferred_element_type=F32))
               * jnp.dot(xb, wu_ref[...], preferred_element_type=F32))
        lane = lax.broadcasted_iota(jnp.int32, (tm, LANES), 1)
        expert = tg_ref[t] * EXPERTS_PER_GROUP + j
        gate = jnp.sum(jnp.where(lane == expert, xg_sc[:, d:], 0.0), -1, keepdims=True)
        acc_sc[...] += jnp.dot((hid * gate).astype(BF16), wd_ref[...], preferred_element_type=F32)

    @pl.when(jnp.logical_and(valid, j == EXPERTS_PER_GROUP - 1))
    def _():
        y_sc[...] = _layer_norm(alpha * xg_sc[:, :d] + acc_sc[...], g2_ref[...], b2_ref[...])
        for_rows(n_real, lambda r: row_out(r).start())
        for_rows(n_real, lambda r: row_out(r).wait())


def _moe(token, tile_group, tile_rows, x1g, wg, wu, wd, g2, b2, *, tm, n_rows, alpha):
    d = x1g.shape[1] - LANES
    d_ff = wg.shape[-1]
    n_tiles = tile_group.shape[0]

    def expert(t, j, tok, tg, tr):
        return tg[t] * EXPERTS_PER_GROUP + jnp.where(tr[t] > 0, j, EXPERTS_PER_GROUP - 1)

    grid_spec = pltpu.PrefetchScalarGridSpec(
        num_scalar_prefetch=3,
        grid=(n_tiles, EXPERTS_PER_GROUP),
        in_specs=[
            pl.BlockSpec(memory_space=pl.ANY),
            pl.BlockSpec((None, d, d_ff), lambda *a: (expert(*a), 0, 0)),
            pl.BlockSpec((None, d, d_ff), lambda *a: (expert(*a), 0, 0)),
            pl.BlockSpec((None, d_ff, d), lambda *a: (expert(*a), 0, 0)),
            pl.BlockSpec((1, d), lambda *a: (0, 0)),
            pl.BlockSpec((1, d), lambda *a: (0, 0)),
        ],
        out_specs=pl.BlockSpec(memory_space=pl.ANY),
        scratch_shapes=[
            pltpu.VMEM((tm, d + LANES), F32),
            pltpu.VMEM((tm, d), BF16),
            pltpu.VMEM((tm, d), F32),
            pltpu.VMEM((tm, d), F32),
            pltpu.SemaphoreType.DMA((2,)),
        ],
    )
    return pl.pallas_call(
        functools.partial(_moe_kernel, tm=tm, d=d, alpha=alpha),
        grid_spec=grid_spec,
        out_shape=jax.ShapeDtypeStruct((n_rows, d), F32),
        compiler_params=_params(2, 48),
        name="moe",
    )(token, tile_group, tile_rows, x1g, wg, wu, wd, g2, b2)


def _sort_by_group(gid, *, tm, n_tiles):
    r = gid.shape[0]
    onehot = (gid[:, None] == jnp.arange(N_GROUPS, dtype=jnp.int32)[None, :]).astype(jnp.int32)
    rank = jnp.sum((jnp.cumsum(onehot, axis=0) - onehot) * onehot, axis=1)
    counts = jnp.sum(onehot, axis=0)
    tiles = (counts + tm - 1) // tm
    tile_end = jnp.cumsum(tiles)
    tile_start = tile_end - tiles
    slot = tile_start[gid] * tm + rank
    token = jnp.zeros((n_tiles * tm,), jnp.int32).at[slot].set(jnp.arange(r, dtype=jnp.int32))
    tile_ids = jnp.arange(n_tiles, dtype=jnp.int32)
    tile_group = jnp.minimum(jnp.sum(tile_ids[:, None] >= tile_end[None, :], axis=1), N_GROUPS - 1).astype(jnp.int32)
    tile_rows = jnp.clip(counts[tile_group] - (tile_ids - tile_start[tile_group]) * tm, 0, tm)
    tile_rows = jnp.where(tile_ids < tile_end[-1], tile_rows, 0).astype(jnp.int32)
    last_group = tile_group[jnp.maximum(tile_end[-1] - 1, 0)]
    tile_group = jnp.where(tile_rows > 0, tile_group, last_group)
    return token, tile_group, tile_rows


def _embed_kernel(x_ref, p_ref, wg_ref, bg_ref, wp_ref, *o_refs, n_prompt_tiles):
    x = x_ref[...]
    gate = jax.nn.sigmoid(jnp.dot(x.astype(BF16), wg_ref[...], preferred_element_type=F32) + bg_ref[...])
    proj = jnp.dot(p_ref[...].astype(BF16), wp_ref[...], preferred_element_type=F32)
    y = x + gate * proj
    if len(o_refs) == 1:
        o_refs[0][...] = y
    else:
        @pl.when(pl.program_id(0) < n_prompt_tiles)
        def _():
            o_refs[0][...] = y

        @pl.when(pl.program_id(0) >= n_prompt_tiles)
        def _():
            o_refs[1][...] = y


def _embed(x2, p, w_pg, b_pg, w_pp, *, tm, n_prompt_rows, split):
    n_rows, d = x2.shape
    d_p = p.shape[1]
    npt = n_prompt_rows // tm
    rows = lambda i: (i, 0)
    if split:
        out_specs = [pl.BlockSpec((tm, d), lambda i: (jnp.minimum(i, npt - 1), 0)),
                     pl.BlockSpec((tm, d), lambda i: (jnp.maximum(i - npt, 0), 0))]
        out_shape = [jax.ShapeDtypeStruct((n_prompt_rows, d), F32),
                     jax.ShapeDtypeStruct((n_rows - n_prompt_rows, d), F32)]
    else:
        out_specs = [pl.BlockSpec((tm, d), rows)]
        out_shape = [jax.ShapeDtypeStruct((n_rows, d), F32)]
    return pl.pallas_call(
        functools.partial(_embed_kernel, n_prompt_tiles=npt),
        grid=(n_rows // tm,),
        in_specs=[
            pl.BlockSpec((tm, d), rows),
            pl.BlockSpec((tm, d_p), rows),
            _resident((d, d)),
            _resident((1, d)),
            _resident((d_p, d)),
        ],
        out_specs=out_specs,
        out_shape=out_shape,
        compiler_params=_params(1, 40),
        name="embed",
    )(x2, p, w_pg, b_pg, w_pp)


def kernel(x_prompt, x_sample, p_prompt, p_sample, cache_k, cache_v, cache_logf, state_conv, page_table, w_in, b_f, g_v, w_s, b_s, conv_w, g_mix, w_out, ln1_g, ln1_b, w_router, b_router, w_gate, w_up, w_down, ln2_g, ln2_b, w_ple_gate, b_ple_gate, w_ple_proj):
    depth = w_in.shape[0]
    batch, seq, d = x_prompt.shape
    n_seq, dec_seq, _ = x_sample.shape
    n_heads = cache_k.shape[3]
    d_a = g_v.shape[-1]
    d_b = n_heads * HEAD_W
    d_c = conv_w.shape[-1]
    np_rows = batch * seq
    ns_rows = n_seq * dec_seq
    n_rows = np_rows + ns_rows
    alpha = (2 * depth) ** 0.25
    tm = 256
    tm_moe = 512
    tq = 512
    assert batch == 1 and CONV_W - 1 <= dec_seq <= CHUNK and dec_seq & (dec_seq - 1) == 0
    assert np_rows % tq == 0 and ns_rows % tm == 0 and np_rows % tm_moe == 0 and ns_rows % tm_moe == 0
    assert w_s.shape[-1] == CHUNK and cache_k.shape[-1] == HEAD_W and n_heads <= SUBLANES
    assert w_gate.shape[1] == N_EXPERTS and w_in.shape[-1] == 2 * d_a + 3 * d_b + n_heads + 3 * d_c
    n_tiles = n_rows // tm_moe + N_GROUPS

    x = jnp.concatenate([x_prompt.reshape(np_rows, d), x_sample.reshape(ns_rows, d)], axis=0)
    wr_pad = jnp.pad(w_router, ((0, 0), (0, LANES - N_EXPERTS)))
    wr_hi = wr_pad.astype(BF16)
    wr_lo = (wr_pad - wr_hi.astype(F32)).astype(BF16)
    br_col = jnp.pad(b_router, (0, LANES - N_EXPERTS)).reshape(LANES, 1)
    reps = CHUNK // dec_seq

    k_prompt = jnp.zeros((depth, np_rows, n_heads, HEAD_W), F32)
    v_prompt = jnp.zeros((depth, np_rows, n_heads, HEAD_W), F32)
    k_sample = jnp.zeros((depth, ns_rows, n_heads, HEAD_W), F32)
    v_sample = jnp.zeros((depth, ns_rows, n_heads, HEAD_W), F32)
    outs = [[] for _ in range(5)]
    for i in range(depth):
        o_q, o_k, o_v = 2 * d_a, 2 * d_a + d_b, 2 * d_a + 2 * d_b
        o_f = 2 * d_a + 3 * d_b
        o_c = o_f + n_heads
        wi = w_in[i]
        w_a = wi[:, :o_q].astype(BF16)
        w_q = wi[:, o_q:o_k].astype(BF16)
        w_k = wi[:, o_k:o_v].astype(BF16)
        w_v = wi[:, o_v:o_f].astype(BF16)
        w_c = jnp.concatenate([wi[:, o_c:], jnp.pad(wi[:, o_f:o_c], ((0, 0), (0, LANES - n_heads)))], axis=1).astype(BF16)
        bf_pad = jnp.pad(b_f[i], (0, LANES - n_heads)).reshape(1, LANES)
        gm = g_mix[i].reshape(1, -1)
        gm_a, gm_b, gm_c = gm[:, :d_a], gm[:, d_a:d_a + d_b], gm[:, d_a + d_b:]
        ws_all = jnp.stack([w_s[i], jnp.tile(w_s[i][:, :dec_seq, :dec_seq], (1, reps, reps))])
        bs_rows = lambda b: jnp.repeat(b.T, HEAD_W, axis=1)
        bs_all = jnp.stack([bs_rows(b_s[i]), bs_rows(jnp.tile(b_s[i][:, :dec_seq], (1, reps)))])
        st = state_conv[i]
        zero_rows = jnp.zeros((n_seq, dec_seq - 1, d_c), F32)
        s1 = jnp.concatenate([st[:, 1:2], zero_rows], axis=1).reshape(ns_rows, d_c)
        s2 = jnp.concatenate([st[:, 0:1], st[:, 1:2], zero_rows[:, 1:]], axis=1).reshape(ns_rows, d_c)

        av_s, ma = _branch_a(x, w_a, g_v[i].reshape(1, d_a), ws_all, bs_all, gm_a,
                             tm=tm, n_prompt_rows=np_rows, dec_seq=dec_seq)
        qt, qs = _project(x, w_q, tm=tq, n_prompt_rows=np_rows, t_scale=ATTN_SCALE * LOG2E, sample_rows=True)
        kb, k_prompt, k_sample = _project(x, w_k, tm=tq, n_prompt_rows=np_rows, flat=True,
                                          cache=(k_prompt, k_sample, i))
        vt, v_prompt, v_sample = _project(x, w_v, tm=tq, n_prompt_rows=np_rows, t_scale=1.0,
                                          cache=(v_prompt, v_sample, i))
        zc, mc, lf, kbias = _branch_c(x, w_c, bf_pad, conv_w[i], gm_c, s1, s2,
                                      tm=tm, n_prompt_rows=np_rows, dec_seq=dec_seq, d_b=d_b)
        mbp = _prompt_attention(qt, kb, kbias, vt, gm_b, n_prompt_rows=np_rows, tq=tq)

        lf_new = lf[np_rows:, :n_heads].reshape(n_seq, dec_seq, n_heads)
        lf_past = cache_logf[i][page_table].reshape(n_seq, -1, n_heads)
        past = lf_past.shape[1]
        lf_t = jnp.concatenate([jnp.swapaxes(lf_past, 1, 2), jnp.swapaxes(lf_new, 1, 2)], axis=2)
        csum = jnp.cumsum(lf_t, axis=2)
        suffix = csum[:, :, -1:] - csum
        mbs = _sample_attention(page_table, qs, k_sample, v_sample, suffix[:, :, :past],
                                jnp.swapaxes(suffix[:, :, past:], 1, 2), gm_b, cache_k, cache_v,
                                layer=i, dec_seq=dec_seq)

        x1g, gid = _mix(ma, mbp, mbs, mc, x, w_out[i].astype(BF16), ln1_g[i].reshape(1, d), ln1_b[i].reshape(1, d),
                        wr_hi, wr_lo, br_col, tm=tm, n_prompt_rows=np_rows, alpha=alpha)
        token, tile_group, tile_rows = _sort_by_group(gid[0], tm=tm_moe, n_tiles=n_tiles)
        x2 = _moe(token, tile_group, tile_rows, x1g, w_gate[i].astype(BF16), w_up[i].astype(BF16),
                  w_down[i].astype(BF16), ln2_g[i].reshape(1, d), ln2_b[i].reshape(1, d),
                  tm=tm_moe, n_rows=n_rows, alpha=alpha)
        p = jnp.concatenate([p_prompt[i].reshape(np_rows, -1), p_sample[i].reshape(ns_rows, -1)], axis=0)
        x = _embed(x2, p, w_ple_gate[i].astype(BF16), b_ple_gate[i].reshape(1, d), w_ple_proj[i].astype(BF16),
                   tm=tm, n_prompt_rows=np_rows, split=i == depth - 1)
        x = x[0] if i < depth - 1 else x

        outs[0].append(lf[:np_rows, :n_heads].reshape(batch, seq, n_heads))
        outs[1].append(zc[np_rows - (CONV_W - 1):np_rows].reshape(batch, CONV_W - 1, d_c))
        outs[2].append(av_s.reshape(n_seq, dec_seq, d_a))
        outs[3].append(lf_new)
        outs[4].append(zc[np_rows:].reshape(n_seq, dec_seq, d_c)[:, dec_seq - (CONV_W - 1):])

    new_lf_p, new_conv_p, new_av_s, new_lf_s, new_conv_s = (jnp.stack(o) for o in outs)
    return (x[0].reshape(batch, seq, d), x[1].reshape(n_seq, dec_seq, d),
            k_prompt.reshape(depth, batch, seq, n_heads, HEAD_W), v_prompt.reshape(depth, batch, seq, n_heads, HEAD_W),
            new_lf_p, new_conv_p, new_av_s,
            k_sample.reshape(depth, n_seq, dec_seq, n_heads, HEAD_W),
            v_sample.reshape(depth, n_seq, dec_seq, n_heads, HEAD_W),
            new_lf_s, new_conv_s)
```

```python
import functools

import numpy as np
import jax
import jax.numpy as jnp
from jax import lax
from jax.experimental import pallas as pl
from jax.experimental.pallas import tpu as pltpu

F32 = jnp.float32
BF16 = jnp.bfloat16

HEAD_W = 128
CHUNK = 128
CONV_W = 3
N_GROUPS = 4
EXPERTS_PER_GROUP = 4
N_EXPERTS = N_GROUPS * EXPERTS_PER_GROUP
LN_EPS = 1e-5
RMS_EPS = 1e-6
ATTN_SCALE = HEAD_W ** -0.5
LOG2E = float(np.log2(np.e))
LANES = 128
SUBLANES = 8
MIB = 1024 * 1024
ROW_DMA_UNROLL = 8


def _params(n_axes, vmem_mib):
    return pltpu.CompilerParams(dimension_semantics=("arbitrary",) * n_axes,
                                vmem_limit_bytes=vmem_mib * MIB)


def _resident(shape):
    zeros = (0,) * len(shape)
    return pl.BlockSpec(shape, lambda *_: zeros, pipeline_mode=pl.Buffered(1))


def _layer_norm(v, gain, bias=None):
    mu = jnp.mean(v, -1, keepdims=True)
    var = jnp.mean(jnp.square(v - mu), -1, keepdims=True)
    y = (v - mu) * lax.rsqrt(var + LN_EPS) * gain
    return y if bias is None else y + bias


def _head_rms(o):
    return o * lax.rsqrt(jnp.mean(o * o, -1, keepdims=True) + RMS_EPS)


def _split3(v):
    hi = v.astype(BF16)
    r1 = v - hi.astype(F32)
    mid = r1.astype(BF16)
    lo = (r1 - mid.astype(F32)).astype(BF16)
    return hi, mid, lo


def _proj_kernel(*refs, n_prompt_tiles, flat, t_scale, sample_rows, cache):
    refs = iter(refs)
    x_ref, w_ref = next(refs), next(refs)
    if cache:
        next(refs), next(refs)
    i = pl.program_id(0)
    z = jnp.dot(x_ref[...].astype(BF16), w_ref[...], preferred_element_type=F32)
    if flat:
        next(refs)[...] = z.astype(BF16)
    if t_scale is not None:
        next(refs)[...] = (z * t_scale).T.astype(BF16)
    if sample_rows:
        next(refs)[...] = z
    if cache:
        prompt_ref, sample_ref = next(refs), next(refs)

        def heads_to(o_ref):
            for h in range(o_ref.shape[1]):
                o_ref[:, h, :] = z[:, h * HEAD_W:(h + 1) * HEAD_W]

        @pl.when(i < n_prompt_tiles)
        def _():
            heads_to(prompt_ref)

        @pl.when(i >= n_prompt_tiles)
        def _():
            heads_to(sample_ref)


def _project(x, w, *, tm, n_prompt_rows, flat=False, t_scale=None, sample_rows=False, cache=None):
    r, d = x.shape
    n = w.shape[1]
    npt = n_prompt_rows // tm
    srows = lambda i: (jnp.maximum(i - npt, 0), 0)
    operands = [x, w]
    in_specs = [pl.BlockSpec((tm, d), lambda i: (i, 0)), _resident((d, n))]
    out_shape, out_specs, aliases = [], [], {}
    if flat:
        out_shape.append(jax.ShapeDtypeStruct((r, n), BF16))
        out_specs.append(pl.BlockSpec((tm, n), lambda i: (i, 0)))
    if t_scale is not None:
        out_shape.append(jax.ShapeDtypeStruct((r // tm, n, tm), BF16))
        out_specs.append(pl.BlockSpec((None, n, tm), lambda i: (i, 0, 0)))
    if sample_rows:
        out_shape.append(jax.ShapeDtypeStruct((r - n_prompt_rows, n), F32))
        out_specs.append(pl.BlockSpec((tm, n), srows))
    if cache is not None:
        prompt_buf, sample_buf, layer = cache
        heads = n // HEAD_W
        for buf, tile in ((prompt_buf, lambda i: jnp.minimum(i, npt - 1)), (sample_buf, lambda i: jnp.maximum(i - npt, 0))):
            aliases[len(operands)] = len(out_shape)
            operands.append(buf)
            in_specs.append(pl.BlockSpec(memory_space=pl.ANY))
            out_shape.append(jax.ShapeDtypeStruct(buf.shape, buf.dtype))
            out_specs.append(pl.BlockSpec((None, tm, heads, HEAD_W), functools.partial(
                lambda i, tile: (layer, tile(i), 0, 0), tile=tile)))
    kern = functools.partial(_proj_kernel, n_prompt_tiles=npt, flat=flat, t_scale=t_scale,
                             sample_rows=sample_rows, cache=cache is not None)
    return pl.pallas_call(
        kern,
        grid=(r // tm,),
        in_specs=in_specs,
        out_specs=out_specs,
        out_shape=out_shape,
        input_output_aliases=aliases,
        compiler_params=_params(1, 40),
        name="proj",
    )(*operands)


def _branch_a_kernel(x_ref, w_ref, gv_ref, ws_ref, bs_ref, gm_ref, av_ref, ma_ref, *,
                     n_prompt_tiles, tm, dec_shift):
    is_sample = pl.program_id(0) >= n_prompt_tiles
    d_a = gv_ref.shape[-1]
    z = jnp.dot(x_ref[...].astype(BF16), w_ref[...], preferred_element_type=F32)
    u = jax.nn.gelu(z[:, :d_a])
    av = _layer_norm(jax.nn.gelu(z[:, d_a:]), gv_ref[...])
    av_ref[...] = av
    avb = av.astype(BF16)
    row = lax.broadcasted_iota(jnp.int32, (CHUNK, CHUNK), 0)
    col = lax.broadcasted_iota(jnp.int32, (CHUNK, CHUNK), 1)
    shift = jnp.where(is_sample, dec_shift, int(np.log2(CHUNK)))
    mask = jnp.logical_and(col <= row, jnp.right_shift(row, shift) == jnp.right_shift(col, shift))
    for h in range(d_a // HEAD_W):
        hs = slice(h * HEAD_W, (h + 1) * HEAD_W)
        wm = jnp.where(mask, ws_ref[h], 0.0).astype(BF16)
        for c in range(tm // CHUNK):
            cs = slice(c * CHUNK, (c + 1) * CHUNK)
            mix = jnp.dot(wm, avb[cs, hs], preferred_element_type=F32) + bs_ref[:, hs]
            o = _head_rms(u[cs, hs] * mix) * gm_ref[:, hs]
            ma_ref[cs, hs] = o.astype(BF16)


def _branch_a(x, w_a, g_v, ws_all, bs_all, gm_a, *, tm, n_prompt_rows, dec_seq):
    r, d = x.shape
    d_a = g_v.shape[-1]
    npt = n_prompt_rows // tm
    n_heads = d_a // HEAD_W
    kern = functools.partial(_branch_a_kernel, n_prompt_tiles=npt, tm=tm, dec_shift=int(np.log2(dec_seq)))
    sel = lambda i: (i >= npt).astype(jnp.int32)
    return pl.pallas_call(
        kern,
        grid=(r // tm,),
        in_specs=[
            pl.BlockSpec((tm, d), lambda i: (i, 0)),
            _resident((d, 2 * d_a)),
            _resident((1, d_a)),
            pl.BlockSpec((None, n_heads, CHUNK, CHUNK), lambda i: (sel(i), 0, 0, 0)),
            pl.BlockSpec((None, CHUNK, d_a), lambda i: (sel(i), 0, 0)),
            _resident((1, d_a)),
        ],
        out_specs=[
            pl.BlockSpec((tm, d_a), lambda i: (jnp.maximum(i - npt, 0), 0)),
            pl.BlockSpec((tm, d_a), lambda i: (i, 0)),
        ],
        out_shape=[
            jax.ShapeDtypeStruct((r - n_prompt_rows, d_a), F32),
            jax.ShapeDtypeStruct((r, d_a), BF16),
        ],
        compiler_params=_params(1, 32),
        name="branch_a",
    )(x, w_a, g_v, ws_all, bs_all, gm_a)


def _branch_c_kernel(x_ref, w_ref, bf_ref, cw_ref, gm_ref, s1_ref, s2_ref,
                     zc_ref, mc_ref, lf_ref, kb_ref, carry_z, carry_c, *,
                     n_prompt_tiles, tm, dec_seq):
    i = pl.program_id(0)
    d_c = gm_ref.shape[-1]

    @pl.when(i == 0)
    def _():
        carry_z[...] = jnp.zeros_like(carry_z)
        carry_c[...] = jnp.zeros_like(carry_c)

    z = jnp.dot(x_ref[...].astype(BF16), w_ref[...], preferred_element_type=F32)
    c_b = z[:, :d_c]
    zc = z[:, d_c:2 * d_c] * z[:, 2 * d_c:3 * d_c]
    zc_ref[...] = zc

    lf = jax.nn.log_sigmoid(z[:, 3 * d_c:] + bf_ref[...])
    lf_ref[...] = lf
    trow = lax.broadcasted_iota(jnp.int32, (tm, tm), 0)
    tcol = lax.broadcasted_iota(jnp.int32, (tm, tm), 1)
    tri = jnp.where(tcol <= trow, 1.0, 0.0).astype(BF16)
    hi, mid, lo = _split3(lf)
    cs = (jnp.dot(tri, hi, preferred_element_type=F32) + jnp.dot(tri, mid, preferred_element_type=F32)
          + jnp.dot(tri, lo, preferred_element_type=F32)) + carry_c[...]
    carry_c[...] = cs[tm - 1:tm, :]
    neg = cs * (-LOG2E)
    lane = lax.broadcasted_iota(jnp.int32, (tm, HEAD_W), 1)
    for h in range(kb_ref.shape[-1] // HEAD_W):
        hi, mid, lo = (t.astype(F32) for t in _split3(jnp.broadcast_to(neg[:, h:h + 1], (tm, HEAD_W))))
        terms = jnp.where(lane == 0, hi, jnp.where(lane == 1, mid, jnp.where(lane == 2, lo, 0.0)))
        kb_ref[:, h * HEAD_W:(h + 1) * HEAD_W] = terms.astype(BF16)

    row = lax.broadcasted_iota(jnp.int32, (tm, d_c), 0)
    back1 = pltpu.roll(zc, 1, 0)
    back2 = pltpu.roll(zc, 2, 0)

    def finish(zm1, zm2):
        y = cw_ref[0:1, :] * zm2 + cw_ref[1:2, :] * zm1 + cw_ref[2:3, :] * zc
        o = c_b * y
        for g in range(d_c // HEAD_W):
            gs = slice(g * HEAD_W, (g + 1) * HEAD_W)
            mc_ref[:, gs] = (_head_rms(o[:, gs]) * gm_ref[:, gs]).astype(BF16)

    @pl.when(i < n_prompt_tiles)
    def _():
        p6 = carry_z[SUBLANES - 2:SUBLANES - 1, :]
        p7 = carry_z[SUBLANES - 1:SUBLANES, :]
        finish(jnp.where(row >= 1, back1, p7),
               jnp.where(row >= 2, back2, jnp.where(row == 0, p6, p7)))

    @pl.when(i >= n_prompt_tiles)
    def _():
        t = jnp.bitwise_and(row, dec_seq - 1)
        finish(jnp.where(t >= 1, back1, s1_ref[...]), jnp.where(t >= 2, back2, s2_ref[...]))

    carry_z[...] = zc[tm - SUBLANES:tm, :]


def _branch_c(x, w_c, bf_pad, conv_w, gm_c, s1, s2, *, tm, n_prompt_rows, dec_seq, d_b):
    r, d = x.shape
    d_c = gm_c.shape[-1]
    n = w_c.shape[1]
    npt = n_prompt_rows // tm
    kern = functools.partial(_branch_c_kernel, n_prompt_tiles=npt, tm=tm, dec_seq=dec_seq)
    rows = lambda i: (i, 0)
    srows = lambda i: (jnp.maximum(i - npt, 0), 0)
    return pl.pallas_call(
        kern,
        grid=(r // tm,),
        in_specs=[
            pl.BlockSpec((tm, d), rows),
            _resident((d, n)),
            _resident((1, LANES)),
            _resident((CONV_W, d_c)),
            _resident((1, d_c)),
            pl.BlockSpec((tm, d_c), srows),
            pl.BlockSpec((tm, d_c), srows),
        ],
        out_specs=[
            pl.BlockSpec((tm, d_c), rows),
            pl.BlockSpec((tm, d_c), rows),
            pl.BlockSpec((tm, LANES), rows),
            pl.BlockSpec((tm, d_b), rows),
        ],
        out_shape=[
            jax.ShapeDtypeStruct((r, d_c), F32),
            jax.ShapeDtypeStruct((r, d_c), BF16),
            jax.ShapeDtypeStruct((r, LANES), F32),
            jax.ShapeDtypeStruct((r, d_b), BF16),
        ],
        scratch_shapes=[pltpu.VMEM((SUBLANES, d_c), F32), pltpu.VMEM((1, LANES), F32)],
        compiler_params=_params(1, 40),
        name="branch_c",
    )(x, w_c, bf_pad, conv_w, gm_c, s1, s2)


def _prompt_attn_kernel(qt_ref, k_ref, kb_ref, vt_ref, gm_ref, o_ref, m_sc, l_sc, acc_sc, *, tq):
    qi = pl.program_id(1)
    n_sub = qt_ref.shape[0]
    sub = lax.broadcasted_iota(jnp.int32, (HEAD_W, tq), 0)
    ones = jnp.where(sub < 3, 1.0, 0.0).astype(BF16)
    q_aug = [jnp.concatenate([qt_ref[s], ones], axis=0) for s in range(n_sub)]
    m_sc[...] = jnp.full_like(m_sc, -jnp.inf)
    l_sc[...] = jnp.zeros_like(l_sc)
    acc_sc[...] = jnp.zeros_like(acc_sc)

    def block(j, s, diagonal):
        k_aug = jnp.concatenate([k_ref[j], kb_ref[j]], axis=1)
        st = jnp.dot(k_aug, q_aug[s], preferred_element_type=F32)
        if diagonal:
            key = lax.broadcasted_iota(jnp.int32, (tq, tq), 0)
            query = lax.broadcasted_iota(jnp.int32, (tq, tq), 1)
            st = jnp.where(key <= query, st, -jnp.inf)
        m_old = m_sc[s]
        m_new = jnp.maximum(m_old, jnp.max(st, axis=0, keepdims=True))
        alpha = jnp.exp2(m_old - m_new)
        pt = jnp.exp2(st - m_new)
        l_sc[s] = alpha * l_sc[s] + jnp.sum(pt, axis=0, keepdims=True)
        acc_sc[s] = alpha * acc_sc[s] + jnp.dot(vt_ref[j], pt.astype(BF16), preferred_element_type=F32)
        m_sc[s] = m_new

    def body(j, carry):
        for s in range(n_sub):
            block(j, s, False)
        return carry

    lax.fori_loop(0, n_sub * qi, body, 0)
    for s in range(n_sub):
        for e in range(s + 1):
            block(n_sub * qi + e, s, e == s)
        o = (acc_sc[s] / l_sc[s]).T
        o_ref[s * tq:(s + 1) * tq, :] = (_head_rms(o) * gm_ref[...]).astype(o_ref.dtype)


def _prompt_attention(qt, kb, kbias, vt, gm_b, *, n_prompt_rows, tq, n_sub):
    d_b = kb.shape[1]
    n_heads = d_b // HEAD_W
    nk = n_prompt_rows // tq
    kb3 = kb.reshape(-1, tq, d_b)
    kbias3 = kbias.reshape(-1, tq, d_b)
    key_rows = pl.BlockSpec((nk, tq, HEAD_W), lambda h, qi: (0, 0, h))
    return pl.pallas_call(
        functools.partial(_prompt_attn_kernel, tq=tq),
        grid=(n_heads, nk // n_sub),
        in_specs=[
            pl.BlockSpec((n_sub, HEAD_W, tq), lambda h, qi: (qi, h, 0)),
            key_rows,
            key_rows,
            pl.BlockSpec((nk, HEAD_W, tq), lambda h, qi: (0, h, 0)),
            pl.BlockSpec((1, HEAD_W), lambda h, qi: (0, h)),
        ],
        out_specs=pl.BlockSpec((n_sub * tq, HEAD_W), lambda h, qi: (qi, h)),
        out_shape=jax.ShapeDtypeStruct((n_prompt_rows, d_b), BF16),
        scratch_shapes=[pltpu.VMEM((n_sub, 1, tq), F32), pltpu.VMEM((n_sub, 1, tq), F32),
                        pltpu.VMEM((n_sub, HEAD_W, tq), F32)],
        compiler_params=_params(2, 32),
        name="prompt_attn",
    )(qt, kb3, kbias3, vt, gm_b)


def _sample_attn_kernel(pt_ref, q_ref, kn_ref, vn_ref, bp_ref, bn_ref, gm_ref, ck_hbm, cv_hbm, o_ref,
                        kbuf, vbuf, kb_sc, vb_sc, s_sc, p_sc, sem, *, layer, n_pages, page, dec_seq):
    n = pl.program_id(0)
    d_b = q_ref.shape[-1]
    n_heads = d_b // HEAD_W
    past = n_pages * page
    n_keys = past + LANES

    def page_copies(seq, slot):
        copies = []
        for p in range(n_pages):
            pg = pt_ref[seq, p]
            rows = pl.ds(p * page * n_heads, page * n_heads)
            copies.append(pltpu.make_async_copy(ck_hbm.at[layer, pg], kbuf.at[slot, rows], sem.at[0, slot]))
            copies.append(pltpu.make_async_copy(cv_hbm.at[layer, pg], vbuf.at[slot, rows], sem.at[1, slot]))
        return copies

    @pl.when(n == 0)
    def _():
        for c in page_copies(0, 0):
            c.start()

    @pl.when(n + 1 < pl.num_programs(0))
    def _():
        for c in page_copies(n + 1, (n + 1) % 2):
            c.start()

    slot = n % 2
    for c in page_copies(n, slot):
        c.wait()

    qrep = jnp.concatenate([q_ref[...]] * (LANES // dec_seq), axis=0)
    rowi = lax.broadcasted_iota(jnp.int32, (LANES, d_b), 0)
    lanei = lax.broadcasted_iota(jnp.int32, (LANES, d_b), 1)
    qbd = jnp.where(lanei // HEAD_W == rowi // dec_seq, qrep, 0.0).astype(BF16)

    pad = jnp.zeros((LANES - dec_seq, HEAD_W), F32)
    for h in range(n_heads):
        hs = slice(h * HEAD_W, (h + 1) * HEAD_W)
        head_rows = pl.ds(h, past, stride=n_heads)
        kb_sc[:past, hs] = kbuf.at[slot][head_rows, :].astype(BF16)
        vb_sc[:past, hs] = vbuf.at[slot][head_rows, :].astype(BF16)
        kb_sc[past:, hs] = jnp.concatenate([kn_ref[:, h, :], pad], axis=0).astype(BF16)
        vb_sc[past:, hs] = jnp.concatenate([vn_ref[:, h, :], pad], axis=0).astype(BF16)

    sc = lax.dot_general(kb_sc[...], qbd, (((1,), (1,)), ((), ())), preferred_element_type=F32) * ATTN_SCALE

    def expand(b):
        col = lax.broadcasted_iota(jnp.int32, (b.shape[0], LANES), 1)
        out = jnp.zeros((b.shape[0], LANES), F32)
        for h in range(n_heads):
            out = jnp.where(col // dec_seq == h, b[:, h:h + 1], out)
        return out

    bp = bp_ref[...]
    pieces = [jnp.broadcast_to(bp[h:h + 1, :], (dec_seq, past)) for h in range(n_heads)]
    if n_heads * dec_seq < LANES:
        pieces.append(jnp.zeros((LANES - n_heads * dec_seq, past), F32))
    b_rep = jnp.concatenate(pieces, axis=0)
    for c in range(past // LANES):
        cs = slice(c * LANES, (c + 1) * LANES)
        s_sc[cs, :] = sc[cs, :] + b_rep[:, cs].T
    krow = lax.broadcasted_iota(jnp.int32, (LANES, LANES), 0)
    qcol = lax.broadcasted_iota(jnp.int32, (LANES, LANES), 1) % dec_seq
    bn = jnp.concatenate([expand(bn_ref[...]), jnp.zeros((LANES - dec_seq, LANES), F32)], axis=0)
    s_sc[past:, :] = jnp.where(krow <= qcol, sc[past:, :] + bn, -jnp.inf)

    m = jnp.max(s_sc[...], axis=0, keepdims=True)
    l = jnp.zeros((LANES, 1), F32)
    for c in range(n_keys // LANES):
        cs = slice(c * LANES, (c + 1) * LANES)
        pt = jnp.exp(s_sc[cs, :] - m).T
        l = l + jnp.sum(pt, -1, keepdims=True)
        p_sc[:, cs] = pt.astype(BF16)
    acc = jnp.dot(p_sc[...], vb_sc[...], preferred_element_type=F32)
    for h in range(n_heads):
        rs = slice(h * dec_seq, (h + 1) * dec_seq)
        hs = slice(h * HEAD_W, (h + 1) * HEAD_W)
        o = acc[rs, hs] / l[rs, :]
        o_ref[:, hs] = _head_rms(o) * gm_ref[:, hs]


def _sample_attention(page_table, qs, k_new, v_new, bias_past, bias_new, gm_b, cache_k, cache_v, *,
                      layer, dec_seq):
    n_seq, n_pages = page_table.shape
    depth, n_pool, page, n_heads, head_w = cache_k.shape
    d_b = n_heads * head_w
    past = n_pages * page
    cache_k = cache_k.reshape(depth, n_pool, page * n_heads, head_w)
    cache_v = cache_v.reshape(depth, n_pool, page * n_heads, head_w)

    new_rows = pl.BlockSpec((None, dec_seq, n_heads, head_w), lambda n, pt: (layer, n, 0, 0))
    grid_spec = pltpu.PrefetchScalarGridSpec(
        num_scalar_prefetch=1,
        grid=(n_seq,),
        in_specs=[
            pl.BlockSpec((dec_seq, d_b), lambda n, pt: (n, 0)),
            new_rows,
            new_rows,
            pl.BlockSpec((None, n_heads, past), lambda n, pt: (n, 0, 0)),
            pl.BlockSpec((None, dec_seq, n_heads), lambda n, pt: (n, 0, 0)),
            pl.BlockSpec((1, d_b), lambda n, pt: (0, 0)),
            pl.BlockSpec(memory_space=pl.ANY),
            pl.BlockSpec(memory_space=pl.ANY),
        ],
        out_specs=pl.BlockSpec((dec_seq, d_b), lambda n, pt: (n, 0)),
        scratch_shapes=[
            pltpu.VMEM((2, past * n_heads, head_w), F32),
            pltpu.VMEM((2, past * n_heads, head_w), F32),
            pltpu.VMEM((past + LANES, d_b), BF16),
            pltpu.VMEM((past + LANES, d_b), BF16),
            pltpu.VMEM((past + LANES, LANES), F32),
            pltpu.VMEM((LANES, past + LANES), BF16),
            pltpu.SemaphoreType.DMA((2, 2)),
        ],
    )
    kern = functools.partial(_sample_attn_kernel, layer=layer, n_pages=n_pages, page=page, dec_seq=dec_seq)
    return pl.pallas_call(
        kern,
        grid_spec=grid_spec,
        out_shape=jax.ShapeDtypeStruct((n_seq * dec_seq, d_b), F32),
        compiler_params=_params(1, 56),
        name="sample_attn",
    )(page_table, qs, k_new, v_new, bias_past, bias_new, gm_b, cache_k, cache_v)


def _mix_kernel(ma_ref, mbp_ref, mbs_ref, mc_ref, x_ref, w_ref, g1_ref, b1_ref, wrh_ref, wrl_ref, br_ref,
                x1g_ref, gid_ref, mixed_sc, *, n_prompt_tiles, alpha):
    i = pl.program_id(0)
    tm, d = x_ref.shape
    d_a = ma_ref.shape[-1]
    d_b = mbp_ref.shape[-1]
    mixed_sc[:, :d_a] = ma_ref[...]
    mixed_sc[:, d_a + d_b:] = mc_ref[...]

    @pl.when(i < n_prompt_tiles)
    def _():
        mixed_sc[:, d_a:d_a + d_b] = mbp_ref[...]

    @pl.when(i >= n_prompt_tiles)
    def _():
        mixed_sc[:, d_a:d_a + d_b] = mbs_ref[...].astype(BF16)

    y = jnp.dot(mixed_sc[...], w_ref[...], preferred_element_type=F32)
    x1 = _layer_norm(alpha * x_ref[...] + y, g1_ref[...], b1_ref[...])
    x1g_ref[:, :d] = x1

    hi = x1.astype(BF16)
    lo = (x1 - hi.astype(F32)).astype(BF16)
    logits = (jnp.dot(hi, wrh_ref[...], preferred_element_type=F32)
              + jnp.dot(lo, wrh_ref[...], preferred_element_type=F32)
              + jnp.dot(hi, wrl_ref[...], preferred_element_type=F32))
    score_all = jax.nn.sigmoid(logits.T)
    sel_all = score_all + br_ref[...]
    score = [score_all[e:e + 1, :] for e in range(N_EXPERTS)]
    sel = [sel_all[e:e + 1, :] for e in range(N_EXPERTS)]

    gscore = []
    for g in range(N_GROUPS):
        a0, a1, a2, a3 = sel[4 * g:4 * g + 4]
        h1, l1 = jnp.maximum(a0, a1), jnp.minimum(a0, a1)
        h2, l2 = jnp.maximum(a2, a3), jnp.minimum(a2, a3)
        gscore.append(jnp.maximum(h1, h2) + jnp.maximum(jnp.minimum(h1, h2), jnp.maximum(l1, l2)))
    gmax = functools.reduce(jnp.maximum, gscore)
    gid = jnp.full(gscore[0].shape, N_GROUPS - 1, jnp.int32)
    for g in range(N_GROUPS - 2, -1, -1):
        gid = jnp.where(gscore[g] == gmax, g, gid)
    gid_ref[...] = gid

    chosen = []
    for e in range(N_EXPERTS):
        g = e // EXPERTS_PER_GROUP
        rank = jnp.zeros(sel[e].shape, jnp.int32)
        for o in range(4 * g, 4 * g + 4):
            if o < e:
                rank = rank + (sel[o] >= sel[e]).astype(jnp.int32)
            elif o > e:
                rank = rank + (sel[o] > sel[e]).astype(jnp.int32)
        chosen.append(jnp.logical_and(gid == g, rank < 2))
    wsum = functools.reduce(jnp.add, [jnp.where(chosen[e], score[e], 0.0) for e in range(N_EXPERTS)])
    rowi = lax.broadcasted_iota(jnp.int32, (N_EXPERTS, tm), 0)
    gates_t = jnp.zeros((N_EXPERTS, tm), F32)
    for e in range(N_EXPERTS):
        gates_t = jnp.where(rowi == e, jnp.where(chosen[e], score[e] / wsum, 0.0), gates_t)
    gates_t = jnp.concatenate([gates_t, jnp.zeros((LANES - N_EXPERTS, tm), F32)], axis=0)
    x1g_ref[:, d:] = gates_t.T


def _mix(ma, mbp, mbs, mc, x, w_out, g1, b1, wr_hi, wr_lo, br_col, *, tm, n_prompt_rows, alpha):
    r, d = x.shape
    d_a, d_b, d_c = ma.shape[1], mbp.shape[1], mc.shape[1]
    npt = n_prompt_rows // tm
    rows = lambda i: (i, 0)
    return pl.pallas_call(
        functools.partial(_mix_kernel, n_prompt_tiles=npt, alpha=alpha),
        grid=(r // tm,),
        in_specs=[
            pl.BlockSpec((tm, d_a), rows),
            pl.BlockSpec((tm, d_b), lambda i: (jnp.minimum(i, npt - 1), 0)),
            pl.BlockSpec((tm, d_b), lambda i: (jnp.maximum(i - npt, 0), 0)),
            pl.BlockSpec((tm, d_c), rows),
            pl.BlockSpec((tm, d), rows),
            _resident((d_a + d_b + d_c, d)),
            _resident((1, d)),
            _resident((1, d)),
            _resident((d, LANES)),
            _resident((d, LANES)),
            _resident((LANES, 1)),
        ],
        out_specs=[
            pl.BlockSpec((tm, d + LANES), rows),
            pl.BlockSpec((1, tm), lambda i: (0, i)),
        ],
        out_shape=[
            jax.ShapeDtypeStruct((r, d + LANES), F32),
            jax.ShapeDtypeStruct((1, r), jnp.int32),
        ],
        scratch_shapes=[pltpu.VMEM((tm, d_a + d_b + d_c), BF16)],
        compiler_params=_params(1, 40),
        name="mix",
    )(ma, mbp, mbs, mc, x, w_out, g1, b1, wr_hi, wr_lo, br_col)


def _moe_kernel(tok_ref, tg_ref, tr_ref, x1g_hbm, wg_ref, wu_ref, wd_ref, g2_ref, b2_ref, out_hbm,
                xg_sc, xb_sc, acc_sc, y_sc, sem, *, tm, d, alpha):
    t = pl.program_id(0)
    j = pl.program_id(1)
    n_real = tr_ref[t]
    valid = n_real > 0

    def row_in(r):
        return pltpu.make_async_copy(x1g_hbm.at[pl.ds(tok_ref[t * tm + r], 1), :], xg_sc.at[pl.ds(r, 1), :],
                                     sem.at[0])

    def row_out(r):
        return pltpu.make_async_copy(y_sc.at[pl.ds(r, 1), :], out_hbm.at[pl.ds(tok_ref[t * tm + r], 1), :],
                                     sem.at[1])

    def for_rows(n, fn, unroll):
        def body(r, carry):
            fn(r)
            return carry
        lax.fori_loop(0, n, body, 0, unroll=unroll)

    @pl.when(jnp.logical_and(valid, j == 0))
    def _():
        for_rows(tm, lambda r: row_in(r).start(), ROW_DMA_UNROLL)
        pltpu.make_async_copy(x1g_hbm.at[pl.ds(0, tm), :], xg_sc, sem.at[0]).wait()
        xb_sc[...] = xg_sc[:, :d].astype(BF16)
        acc_sc[...] = jnp.zeros_like(acc_sc)

    @pl.when(valid)
    def _():
        xb = xb_sc[...]
        hid = (jax.nn.silu(jnp.dot(xb, wg_ref[...].astype(BF16), preferred_element_type=F32))
               * jnp.dot(xb, wu_ref[...].astype(BF16), preferred_element_type=F32))
        lane = lax.broadcasted_iota(jnp.int32, (tm, LANES), 1)
        expert = tg_ref[t] * EXPERTS_PER_GROUP + j
        gate = jnp.sum(jnp.where(lane == expert, xg_sc[:, d:], 0.0), -1, keepdims=True)
        acc_sc[...] += jnp.dot((hid * gate).astype(BF16), wd_ref[...].astype(BF16), preferred_element_type=F32)

    @pl.when(jnp.logical_and(valid, j == EXPERTS_PER_GROUP - 1))
    def _():
        y_sc[...] = _layer_norm(alpha * xg_sc[:, :d] + acc_sc[...], g2_ref[...], b2_ref[...])
        for_rows(n_real, lambda r: row_out(r).start(), 1)
        for_rows(n_real, lambda r: row_out(r).wait(), 1)


def _moe(token, tile_group, tile_rows, x1g, wg, wu, wd, g2, b2, *, layer, tm, n_rows, alpha):
    d = x1g.shape[1] - LANES
    d_ff = wg.shape[-1]
    n_tiles = tile_group.shape[0]

    def expert(t, j, tok, tg, tr):
        return layer, tg[t] * EXPERTS_PER_GROUP + jnp.where(tr[t] > 0, j, EXPERTS_PER_GROUP - 1), 0, 0

    grid_spec = pltpu.PrefetchScalarGridSpec(
        num_scalar_prefetch=3,
        grid=(n_tiles, EXPERTS_PER_GROUP),
        in_specs=[
            pl.BlockSpec(memory_space=pl.ANY),
            pl.BlockSpec((None, None, d, d_ff), expert),
            pl.BlockSpec((None, None, d, d_ff), expert),
            pl.BlockSpec((None, None, d_ff, d), expert),
            pl.BlockSpec((1, d), lambda *a: (0, 0)),
            pl.BlockSpec((1, d), lambda *a: (0, 0)),
        ],
        out_specs=pl.BlockSpec(memory_space=pl.ANY),
        scratch_shapes=[
            pltpu.VMEM((tm, d + LANES), F32),
            pltpu.VMEM((tm, d), BF16),
            pltpu.VMEM((tm, d), F32),
            pltpu.VMEM((tm, d), F32),
            pltpu.SemaphoreType.DMA((2,)),
        ],
    )
    return pl.pallas_call(
        functools.partial(_moe_kernel, tm=tm, d=d, alpha=alpha),
        grid_spec=grid_spec,
        out_shape=jax.ShapeDtypeStruct((n_rows, d), F32),
        compiler_params=_params(2, 56),
        name="moe",
    )(token, tile_group, tile_rows, x1g, wg, wu, wd, g2, b2)


def _sort_by_group(gid, *, tm, n_tiles):
    r = gid.shape[0]
    onehot = (gid[:, None] == jnp.arange(N_GROUPS, dtype=jnp.int32)[None, :]).astype(jnp.int32)
    rank = jnp.sum((jnp.cumsum(onehot, axis=0) - onehot) * onehot, axis=1)
    counts = jnp.sum(onehot, axis=0)
    tiles = (counts + tm - 1) // tm
    tile_end = jnp.cumsum(tiles)
    tile_start = tile_end - tiles
    slot = tile_start[gid] * tm + rank
    token = jnp.zeros((n_tiles * tm,), jnp.int32).at[slot].set(jnp.arange(r, dtype=jnp.int32))
    tile_ids = jnp.arange(n_tiles, dtype=jnp.int32)
    tile_group = jnp.minimum(jnp.sum(tile_ids[:, None] >= tile_end[None, :], axis=1), N_GROUPS - 1).astype(jnp.int32)
    tile_rows = jnp.clip(counts[tile_group] - (tile_ids - tile_start[tile_group]) * tm, 0, tm)
    tile_rows = jnp.where(tile_ids < tile_end[-1], tile_rows, 0).astype(jnp.int32)
    last_group = tile_group[jnp.maximum(tile_end[-1] - 1, 0)]
    tile_group = jnp.where(tile_rows > 0, tile_group, last_group)
    return token, tile_group, tile_rows


def _embed_kernel(x_ref, p_ref, wg_ref, bg_ref, wp_ref, *o_refs, n_prompt_tiles):
    x = x_ref[...]
    gate = jax.nn.sigmoid(jnp.dot(x.astype(BF16), wg_ref[...], preferred_element_type=F32) + bg_ref[...])
    proj = jnp.dot(p_ref[...].astype(BF16), wp_ref[...], preferred_element_type=F32)
    y = x + gate * proj
    if len(o_refs) == 1:
        o_refs[0][...] = y
    else:
        @pl.when(pl.program_id(0) < n_prompt_tiles)
        def _():
            o_refs[0][...] = y

        @pl.when(pl.program_id(0) >= n_prompt_tiles)
        def _():
            o_refs[1][...] = y


def _embed(x2, p, w_pg, b_pg, w_pp, *, tm, n_prompt_rows, split):
    n_rows, d = x2.shape
    d_p = p.shape[1]
    npt = n_prompt_rows // tm
    rows = lambda i: (i, 0)
    if split:
        out_specs = [pl.BlockSpec((tm, d), lambda i: (jnp.minimum(i, npt - 1), 0)),
                     pl.BlockSpec((tm, d), lambda i: (jnp.maximum(i - npt, 0), 0))]
        out_shape = [jax.ShapeDtypeStruct((n_prompt_rows, d), F32),
                     jax.ShapeDtypeStruct((n_rows - n_prompt_rows, d), F32)]
    else:
        out_specs = [pl.BlockSpec((tm, d), rows)]
        out_shape = [jax.ShapeDtypeStruct((n_rows, d), F32)]
    return pl.pallas_call(
        functools.partial(_embed_kernel, n_prompt_tiles=npt),
        grid=(n_rows // tm,),
        in_specs=[
            pl.BlockSpec((tm, d), rows),
            pl.BlockSpec((tm, d_p), rows),
            _resident((d, d)),
            _resident((1, d)),
            _resident((d_p, d)),
        ],
        out_specs=out_specs,
        out_shape=out_shape,
        compiler_params=_params(1, 40),
        name="embed",
    )(x2, p, w_pg, b_pg, w_pp)


def kernel(x_prompt, x_sample, p_prompt, p_sample, cache_k, cache_v, cache_logf, state_conv, page_table, w_in, b_f, g_v, w_s, b_s, conv_w, g_mix, w_out, ln1_g, ln1_b, w_router, b_router, w_gate, w_up, w_down, ln2_g, ln2_b, w_ple_gate, b_ple_gate, w_ple_proj):
    depth = w_in.shape[0]
    batch, seq, d = x_prompt.shape
    n_seq, dec_seq, _ = x_sample.shape
    n_heads = cache_k.shape[3]
    d_a = g_v.shape[-1]
    d_b = n_heads * HEAD_W
    d_c = conv_w.shape[-1]
    np_rows = batch * seq
    ns_rows = n_seq * dec_seq
    n_rows = np_rows + ns_rows
    alpha = (2 * depth) ** 0.25
    tm = 256
    tm_moe = 512
    tq = 512
    assert batch == 1 and CONV_W - 1 <= dec_seq <= CHUNK and dec_seq & (dec_seq - 1) == 0
    assert np_rows % tq == 0 and ns_rows % tm == 0 and np_rows % tm_moe == 0 and ns_rows % tm_moe == 0
    assert w_s.shape[-1] == CHUNK and cache_k.shape[-1] == HEAD_W and n_heads <= SUBLANES
    assert w_gate.shape[1] == N_EXPERTS and w_in.shape[-1] == 2 * d_a + 3 * d_b + n_heads + 3 * d_c
    n_tiles = n_rows // tm_moe + N_GROUPS

    x = jnp.concatenate([x_prompt.reshape(np_rows, d), x_sample.reshape(ns_rows, d)], axis=0)
    wr_pad = jnp.pad(w_router, ((0, 0), (0, LANES - N_EXPERTS)))
    wr_hi = wr_pad.astype(BF16)
    wr_lo = (wr_pad - wr_hi.astype(F32)).astype(BF16)
    br_col = jnp.pad(b_router, (0, LANES - N_EXPERTS)).reshape(LANES, 1)
    reps = CHUNK // dec_seq

    k_prompt = jnp.zeros((depth, np_rows, n_heads, HEAD_W), F32)
    v_prompt = jnp.zeros((depth, np_rows, n_heads, HEAD_W), F32)
    k_sample = jnp.zeros((depth, ns_rows, n_heads, HEAD_W), F32)
    v_sample = jnp.zeros((depth, ns_rows, n_heads, HEAD_W), F32)
    outs = [[] for _ in range(5)]
    for i in range(depth):
        o_q, o_k, o_v = 2 * d_a, 2 * d_a + d_b, 2 * d_a + 2 * d_b
        o_f = 2 * d_a + 3 * d_b
        o_c = o_f + n_heads
        wi = w_in[i]
        w_a = wi[:, :o_q].astype(BF16)
        w_q = wi[:, o_q:o_k].astype(BF16)
        w_k = wi[:, o_k:o_v].astype(BF16)
        w_v = wi[:, o_v:o_f].astype(BF16)
        w_c = jnp.concatenate([wi[:, o_c:], jnp.pad(wi[:, o_f:o_c], ((0, 0), (0, LANES - n_heads)))], axis=1).astype(BF16)
        bf_pad = jnp.pad(b_f[i], (0, LANES - n_heads)).reshape(1, LANES)
        gm = g_mix[i].reshape(1, -1)
        gm_a, gm_b, gm_c = gm[:, :d_a], gm[:, d_a:d_a + d_b], gm[:, d_a + d_b:]
        ws_all = jnp.stack([w_s[i], jnp.tile(w_s[i][:, :dec_seq, :dec_seq], (1, reps, reps))])
        bs_rows = lambda b: jnp.repeat(b.T, HEAD_W, axis=1)
        bs_all = jnp.stack([bs_rows(b_s[i]), bs_rows(jnp.tile(b_s[i][:, :dec_seq], (1, reps)))])
        st = state_conv[i]
        zero_rows = jnp.zeros((n_seq, dec_seq - 1, d_c), F32)
        s1 = jnp.concatenate([st[:, 1:2], zero_rows], axis=1).reshape(ns_rows, d_c)
        s2 = jnp.concatenate([st[:, 0:1], st[:, 1:2], zero_rows[:, 1:]], axis=1).reshape(ns_rows, d_c)

        av_s, ma = _branch_a(x, w_a, g_v[i].reshape(1, d_a), ws_all, bs_all, gm_a,
                             tm=tm, n_prompt_rows=np_rows, dec_seq=dec_seq)
        qt, qs = _project(x, w_q, tm=tq, n_prompt_rows=np_rows, t_scale=ATTN_SCALE * LOG2E, sample_rows=True)
        kb, k_prompt, k_sample = _project(x, w_k, tm=tq, n_prompt_rows=np_rows, flat=True,
                                          cache=(k_prompt, k_sample, i))
        vt, v_prompt, v_sample = _project(x, w_v, tm=tq, n_prompt_rows=np_rows, t_scale=1.0,
                                          cache=(v_prompt, v_sample, i))
        zc, mc, lf, kbias = _branch_c(x, w_c, bf_pad, conv_w[i], gm_c, s1, s2,
                                      tm=tm, n_prompt_rows=np_rows, dec_seq=dec_seq, d_b=d_b)
        mbp = _prompt_attention(qt, kb, kbias, vt, gm_b, n_prompt_rows=np_rows, tq=tq, n_sub=2)

        lf_new = lf[np_rows:, :n_heads].reshape(n_seq, dec_seq, n_heads)
        pages_t = jnp.swapaxes(cache_logf[i], 1, 2)[page_table]
        lf_past_t = jnp.swapaxes(pages_t, 1, 2).reshape(n_seq, n_heads, -1)
        past = lf_past_t.shape[2]
        lf_t = jnp.concatenate([lf_past_t, jnp.swapaxes(lf_new, 1, 2)], axis=2)
        csum = jnp.cumsum(lf_t, axis=2)
        suffix = csum[:, :, -1:] - csum
        mbs = _sample_attention(page_table, qs, k_sample, v_sample, suffix[:, :, :past],
                                jnp.swapaxes(suffix[:, :, past:], 1, 2), gm_b, cache_k, cache_v,
                                layer=i, dec_seq=dec_seq)

        x1g, gid = _mix(ma, mbp, mbs, mc, x, w_out[i].astype(BF16), ln1_g[i].reshape(1, d), ln1_b[i].reshape(1, d),
                        wr_hi, wr_lo, br_col, tm=tm, n_prompt_rows=np_rows, alpha=alpha)
        token, tile_group, tile_rows = _sort_by_group(gid[0], tm=tm_moe, n_tiles=n_tiles)
        x2 = _moe(token, tile_group, tile_rows, x1g, w_gate, w_up, w_down,
                  ln2_g[i].reshape(1, d), ln2_b[i].reshape(1, d),
                  layer=i, tm=tm_moe, n_rows=n_rows, alpha=alpha)
        p = jnp.concatenate([p_prompt[i].reshape(np_rows, -1), p_sample[i].reshape(ns_rows, -1)], axis=0)
        x = _embed(x2, p, w_ple_gate[i].astype(BF16), b_ple_gate[i].reshape(1, d), w_ple_proj[i].astype(BF16),
                   tm=tm, n_prompt_rows=np_rows, split=i == depth - 1)
        x = x[0] if i < depth - 1 else x

        outs[0].append(lf[:np_rows, :n_heads].reshape(batch, seq, n_heads))
        outs[1].append(zc[np_rows - (CONV_W - 1):np_rows].reshape(batch, CONV_W - 1, d_c))
        outs[2].append(av_s.reshape(n_seq, dec_seq, d_a))
        outs[3].append(lf_new)
        outs[4].append(zc[np_rows:].reshape(n_seq, dec_seq, d_c)[:, dec_seq - (CONV_W - 1):])

    new_lf_p, new_conv_p, new_av_s, new_lf_s, new_conv_s = (jnp.stack(o) for o in outs)
    return (x[0].reshape(batch, seq, d), x[1].reshape(n_seq, dec_seq, d),
            k_prompt.reshape(depth, batch, seq, n_heads, HEAD_W), v_prompt.reshape(depth, batch, seq, n_heads, HEAD_W),
            new_lf_p, new_conv_p, new_av_s,
            k_sample.reshape(depth, n_seq, dec_seq, n_heads, HEAD_W),
            v_sample.reshape(depth, n_seq, dec_seq, n_heads, HEAD_W),
            new_lf_s, new_conv_s)
```

```python
import functools

import numpy as np
import jax
import jax.numpy as jnp
from jax import lax
from jax.experimental import pallas as pl
from jax.experimental.pallas import tpu as pltpu

F32 = jnp.float32
BF16 = jnp.bfloat16

HEAD_W = 128
CHUNK = 128
CONV_W = 3
N_GROUPS = 4
EXPERTS_PER_GROUP = 4
N_EXPERTS = N_GROUPS * EXPERTS_PER_GROUP
LN_EPS = 1e-5
RMS_EPS = 1e-6
ATTN_SCALE = HEAD_W ** -0.5
LOG2E = float(np.log2(np.e))
LANES = 128
SUBLANES = 8
MIB = 1024 * 1024
ROW_DMA_UNROLL = 8


def _params(n_axes, vmem_mib):
    return pltpu.CompilerParams(dimension_semantics=("arbitrary",) * n_axes,
                                vmem_limit_bytes=vmem_mib * MIB)


def _resident(shape):
    zeros = (0,) * len(shape)
    return pl.BlockSpec(shape, lambda *_: zeros, pipeline_mode=pl.Buffered(1))


def _layer_norm(v, gain, bias=None):
    mu = jnp.mean(v, -1, keepdims=True)
    var = jnp.mean(jnp.square(v - mu), -1, keepdims=True)
    y = (v - mu) * lax.rsqrt(var + LN_EPS) * gain
    return y if bias is None else y + bias


def _head_rms(o):
    return o * lax.rsqrt(jnp.mean(o * o, -1, keepdims=True) + RMS_EPS)


def _split3(v):
    hi = v.astype(BF16)
    r1 = v - hi.astype(F32)
    mid = r1.astype(BF16)
    lo = (r1 - mid.astype(F32)).astype(BF16)
    return hi, mid, lo


def _proj_kernel(*refs, n_prompt_tiles, flat, t_scale, sample_rows, cache):
    refs = iter(refs)
    x_ref, w_ref = next(refs), next(refs)
    if cache:
        next(refs), next(refs)
    i = pl.program_id(0)
    z = jnp.dot(x_ref[...].astype(BF16), w_ref[...], preferred_element_type=F32)
    if flat:
        next(refs)[...] = z.astype(BF16)
    if t_scale is not None:
        next(refs)[...] = (z * t_scale).T.astype(BF16)
    if sample_rows:
        next(refs)[...] = z
    if cache:
        prompt_ref, sample_ref = next(refs), next(refs)

        def heads_to(o_ref):
            for h in range(o_ref.shape[1]):
                o_ref[:, h, :] = z[:, h * HEAD_W:(h + 1) * HEAD_W]

        @pl.when(i < n_prompt_tiles)
        def _():
            heads_to(prompt_ref)

        @pl.when(i >= n_prompt_tiles)
        def _():
            heads_to(sample_ref)


def _project(x, w, *, tm, n_prompt_rows, flat=False, t_scale=None, sample_rows=False, cache=None):
    r, d = x.shape
    n = w.shape[1]
    npt = n_prompt_rows // tm
    srows = lambda i: (jnp.maximum(i - npt, 0), 0)
    operands = [x, w]
    in_specs = [pl.BlockSpec((tm, d), lambda i: (i, 0)), _resident((d, n))]
    out_shape, out_specs, aliases = [], [], {}
    if flat:
        out_shape.append(jax.ShapeDtypeStruct((r, n), BF16))
        out_specs.append(pl.BlockSpec((tm, n), lambda i: (i, 0)))
    if t_scale is not None:
        out_shape.append(jax.ShapeDtypeStruct((r // tm, n, tm), BF16))
        out_specs.append(pl.BlockSpec((None, n, tm), lambda i: (i, 0, 0)))
    if sample_rows:
        out_shape.append(jax.ShapeDtypeStruct((r - n_prompt_rows, n), F32))
        out_specs.append(pl.BlockSpec((tm, n), srows))
    if cache is not None:
        prompt_buf, sample_buf, layer = cache
        heads = n // HEAD_W
        for buf, tile in ((prompt_buf, lambda i: jnp.minimum(i, npt - 1)), (sample_buf, lambda i: jnp.maximum(i - npt, 0))):
            aliases[len(operands)] = len(out_shape)
            operands.append(buf)
            in_specs.append(pl.BlockSpec(memory_space=pl.ANY))
            out_shape.append(jax.ShapeDtypeStruct(buf.shape, buf.dtype))
            out_specs.append(pl.BlockSpec((None, tm, heads, HEAD_W), functools.partial(
                lambda i, tile: (layer, tile(i), 0, 0), tile=tile)))
    kern = functools.partial(_proj_kernel, n_prompt_tiles=npt, flat=flat, t_scale=t_scale,
                             sample_rows=sample_rows, cache=cache is not None)
    return pl.pallas_call(
        kern,
        grid=(r // tm,),
        in_specs=in_specs,
        out_specs=out_specs,
        out_shape=out_shape,
        input_output_aliases=aliases,
        compiler_params=_params(1, 40),
        name="proj",
    )(*operands)


def _branch_a_kernel(x_ref, w_ref, gv_ref, ws_ref, bs_ref, gm_ref, av_ref, ma_ref, *,
                     n_prompt_tiles, tm, dec_shift):
    is_sample = pl.program_id(0) >= n_prompt_tiles
    d_a = gv_ref.shape[-1]
    z = jnp.dot(x_ref[...].astype(BF16), w_ref[...], preferred_element_type=F32)
    u = jax.nn.gelu(z[:, :d_a])
    av = _layer_norm(jax.nn.gelu(z[:, d_a:]), gv_ref[...])
    av_ref[...] = av
    avb = av.astype(BF16)
    row = lax.broadcasted_iota(jnp.int32, (CHUNK, CHUNK), 0)
    col = lax.broadcasted_iota(jnp.int32, (CHUNK, CHUNK), 1)
    shift = jnp.where(is_sample, dec_shift, int(np.log2(CHUNK)))
    mask = jnp.logical_and(col <= row, jnp.right_shift(row, shift) == jnp.right_shift(col, shift))
    for h in range(d_a // HEAD_W):
        hs = slice(h * HEAD_W, (h + 1) * HEAD_W)
        wm = jnp.where(mask, ws_ref[h], 0.0).astype(BF16)
        for c in range(tm // CHUNK):
            cs = slice(c * CHUNK, (c + 1) * CHUNK)
            mix = jnp.dot(wm, avb[cs, hs], preferred_element_type=F32) + bs_ref[:, hs]
            o = _head_rms(u[cs, hs] * mix) * gm_ref[:, hs]
            ma_ref[cs, hs] = o.astype(BF16)


def _branch_a(x, w_a, g_v, ws_all, bs_all, gm_a, *, tm, n_prompt_rows, dec_seq):
    r, d = x.shape
    d_a = g_v.shape[-1]
    npt = n_prompt_rows // tm
    n_heads = d_a // HEAD_W
    kern = functools.partial(_branch_a_kernel, n_prompt_tiles=npt, tm=tm, dec_shift=int(np.log2(dec_seq)))
    sel = lambda i: (i >= npt).astype(jnp.int32)
    return pl.pallas_call(
        kern,
        grid=(r // tm,),
        in_specs=[
            pl.BlockSpec((tm, d), lambda i: (i, 0)),
            _resident((d, 2 * d_a)),
            _resident((1, d_a)),
            pl.BlockSpec((None, n_heads, CHUNK, CHUNK), lambda i: (sel(i), 0, 0, 0)),
            pl.BlockSpec((None, CHUNK, d_a), lambda i: (sel(i), 0, 0)),
            _resident((1, d_a)),
        ],
        out_specs=[
            pl.BlockSpec((tm, d_a), lambda i: (jnp.maximum(i - npt, 0), 0)),
            pl.BlockSpec((tm, d_a), lambda i: (i, 0)),
        ],
        out_shape=[
            jax.ShapeDtypeStruct((r - n_prompt_rows, d_a), F32),
            jax.ShapeDtypeStruct((r, d_a), BF16),
        ],
        compiler_params=_params(1, 32),
        name="branch_a",
    )(x, w_a, g_v, ws_all, bs_all, gm_a)


def _branch_c_kernel(x_ref, w_ref, bf_ref, cw_ref, gm_ref, s1_ref, s2_ref,
                     zc_ref, mc_ref, lf_ref, kb_ref, carry_z, carry_c, *,
                     n_prompt_tiles, tm, dec_seq):
    i = pl.program_id(0)
    d_c = gm_ref.shape[-1]

    @pl.when(i == 0)
    def _():
        carry_z[...] = jnp.zeros_like(carry_z)
        carry_c[...] = jnp.zeros_like(carry_c)

    z = jnp.dot(x_ref[...].astype(BF16), w_ref[...], preferred_element_type=F32)
    c_b = z[:, :d_c]
    zc = z[:, d_c:2 * d_c] * z[:, 2 * d_c:3 * d_c]
    zc_ref[...] = zc

    lf = jax.nn.log_sigmoid(z[:, 3 * d_c:] + bf_ref[...])
    lf_ref[...] = lf
    trow = lax.broadcasted_iota(jnp.int32, (tm, tm), 0)
    tcol = lax.broadcasted_iota(jnp.int32, (tm, tm), 1)
    tri = jnp.where(tcol <= trow, 1.0, 0.0).astype(BF16)
    hi, mid, lo = _split3(lf)
    cs = (jnp.dot(tri, hi, preferred_element_type=F32) + jnp.dot(tri, mid, preferred_element_type=F32)
          + jnp.dot(tri, lo, preferred_element_type=F32)) + carry_c[...]
    carry_c[...] = cs[tm - 1:tm, :]
    neg = cs * (-LOG2E)
    lane = lax.broadcasted_iota(jnp.int32, (tm, HEAD_W), 1)
    for h in range(kb_ref.shape[-1] // HEAD_W):
        hi, mid, lo = (t.astype(F32) for t in _split3(jnp.broadcast_to(neg[:, h:h + 1], (tm, HEAD_W))))
        terms = jnp.where(lane == 0, hi, jnp.where(lane == 1, mid, jnp.where(lane == 2, lo, 0.0)))
        kb_ref[:, h * HEAD_W:(h + 1) * HEAD_W] = terms.astype(BF16)

    row = lax.broadcasted_iota(jnp.int32, (tm, d_c), 0)
    back1 = pltpu.roll(zc, 1, 0)
    back2 = pltpu.roll(zc, 2, 0)

    def finish(zm1, zm2):
        y = cw_ref[0:1, :] * zm2 + cw_ref[1:2, :] * zm1 + cw_ref[2:3, :] * zc
        o = c_b * y
        for g in range(d_c // HEAD_W):
            gs = slice(g * HEAD_W, (g + 1) * HEAD_W)
            mc_ref[:, gs] = (_head_rms(o[:, gs]) * gm_ref[:, gs]).astype(BF16)

    @pl.when(i < n_prompt_tiles)
    def _():
        p6 = carry_z[SUBLANES - 2:SUBLANES - 1, :]
        p7 = carry_z[SUBLANES - 1:SUBLANES, :]
        finish(jnp.where(row >= 1, back1, p7),
               jnp.where(row >= 2, back2, jnp.where(row == 0, p6, p7)))

    @pl.when(i >= n_prompt_tiles)
    def _():
        t = jnp.bitwise_and(row, dec_seq - 1)
        finish(jnp.where(t >= 1, back1, s1_ref[...]), jnp.where(t >= 2, back2, s2_ref[...]))

    carry_z[...] = zc[tm - SUBLANES:tm, :]


def _branch_c(x, w_c, bf_pad, conv_w, gm_c, s1, s2, *, tm, n_prompt_rows, dec_seq, d_b):
    r, d = x.shape
    d_c = gm_c.shape[-1]
    n = w_c.shape[1]
    npt = n_prompt_rows // tm
    kern = functools.partial(_branch_c_kernel, n_prompt_tiles=npt, tm=tm, dec_seq=dec_seq)
    rows = lambda i: (i, 0)
    srows = lambda i: (jnp.maximum(i - npt, 0), 0)
    return pl.pallas_call(
        kern,
        grid=(r // tm,),
        in_specs=[
            pl.BlockSpec((tm, d), rows),
            _resident((d, n)),
            _resident((1, LANES)),
            _resident((CONV_W, d_c)),
            _resident((1, d_c)),
            pl.BlockSpec((tm, d_c), srows),
            pl.BlockSpec((tm, d_c), srows),
        ],
        out_specs=[
            pl.BlockSpec((tm, d_c), rows),
            pl.BlockSpec((tm, d_c), rows),
            pl.BlockSpec((tm, LANES), rows),
            pl.BlockSpec((tm, d_b), rows),
        ],
        out_shape=[
            jax.ShapeDtypeStruct((r, d_c), F32),
            jax.ShapeDtypeStruct((r, d_c), BF16),
            jax.ShapeDtypeStruct((r, LANES), F32),
            jax.ShapeDtypeStruct((r, d_b), BF16),
        ],
        scratch_shapes=[pltpu.VMEM((SUBLANES, d_c), F32), pltpu.VMEM((1, LANES), F32)],
        compiler_params=_params(1, 40),
        name="branch_c",
    )(x, w_c, bf_pad, conv_w, gm_c, s1, s2)


def _prompt_attn_kernel(qt_ref, k_ref, kb_ref, vt_ref, gm_ref, o_ref, m_sc, l_sc, acc_sc, *, tq):
    qi = pl.program_id(1)
    n_sub = qt_ref.shape[0]
    sub = lax.broadcasted_iota(jnp.int32, (HEAD_W, tq), 0)
    ones = jnp.where(sub < 3, 1.0, 0.0).astype(BF16)
    q_aug = [jnp.concatenate([qt_ref[s], ones], axis=0) for s in range(n_sub)]
    m_sc[...] = jnp.full_like(m_sc, -jnp.inf)
    l_sc[...] = jnp.zeros_like(l_sc)
    acc_sc[...] = jnp.zeros_like(acc_sc)

    def block(j, s, diagonal):
        k_aug = jnp.concatenate([k_ref[j], kb_ref[j]], axis=1)
        st = jnp.dot(k_aug, q_aug[s], preferred_element_type=F32)
        if diagonal:
            key = lax.broadcasted_iota(jnp.int32, (tq, tq), 0)
            query = lax.broadcasted_iota(jnp.int32, (tq, tq), 1)
            st = jnp.where(key <= query, st, -jnp.inf)
        m_old = m_sc[s]
        m_new = jnp.maximum(m_old, jnp.max(st, axis=0, keepdims=True))
        alpha = jnp.exp2(m_old - m_new)
        pt = jnp.exp2(st - m_new)
        l_sc[s] = alpha * l_sc[s] + jnp.sum(pt, axis=0, keepdims=True)
        acc_sc[s] = alpha * acc_sc[s] + jnp.dot(vt_ref[j], pt.astype(BF16), preferred_element_type=F32)
        m_sc[s] = m_new

    def body(j, carry):
        for s in range(n_sub):
            block(j, s, False)
        return carry

    lax.fori_loop(0, n_sub * qi, body, 0)
    for s in range(n_sub):
        for e in range(s + 1):
            block(n_sub * qi + e, s, e == s)
        o = (acc_sc[s] / l_sc[s]).T
        o_ref[s * tq:(s + 1) * tq, :] = (_head_rms(o) * gm_ref[...]).astype(o_ref.dtype)


def _prompt_attention(qt, kb, kbias, vt, gm_b, *, n_prompt_rows, tq, n_sub):
    d_b = kb.shape[1]
    n_heads = d_b // HEAD_W
    nk = n_prompt_rows // tq
    kb3 = kb.reshape(-1, tq, d_b)
    kbias3 = kbias.reshape(-1, tq, d_b)
    key_rows = pl.BlockSpec((nk, tq, HEAD_W), lambda h, qi: (0, 0, h))
    return pl.pallas_call(
        functools.partial(_prompt_attn_kernel, tq=tq),
        grid=(n_heads, nk // n_sub),
        in_specs=[
            pl.BlockSpec((n_sub, HEAD_W, tq), lambda h, qi: (qi, h, 0)),
            key_rows,
            key_rows,
            pl.BlockSpec((nk, HEAD_W, tq), lambda h, qi: (0, h, 0)),
            pl.BlockSpec((1, HEAD_W), lambda h, qi: (0, h)),
        ],
        out_specs=pl.BlockSpec((n_sub * tq, HEAD_W), lambda h, qi: (qi, h)),
        out_shape=jax.ShapeDtypeStruct((n_prompt_rows, d_b), BF16),
        scratch_shapes=[pltpu.VMEM((n_sub, 1, tq), F32), pltpu.VMEM((n_sub, 1, tq), F32),
                        pltpu.VMEM((n_sub, HEAD_W, tq), F32)],
        compiler_params=_params(2, 32),
        name="prompt_attn",
    )(qt, kb3, kbias3, vt, gm_b)


def _suffix_kernel(lf_ref, tail_ref, o_ref):
    past = lf_ref.shape[1]
    u = lax.broadcasted_iota(jnp.int32, (LANES, LANES), 0)
    s = lax.broadcasted_iota(jnp.int32, (LANES, LANES), 1)
    later = jnp.where(u > s, 1.0, 0.0).astype(BF16)
    carry = tail_ref[...]
    for c in reversed(range(past // LANES)):
        cs = slice(c * LANES, (c + 1) * LANES)
        x = lf_ref[:, cs]
        hi, mid, lo = _split3(x)
        within = (jnp.dot(hi, later, preferred_element_type=F32) + jnp.dot(mid, later, preferred_element_type=F32)
                  + jnp.dot(lo, later, preferred_element_type=F32))
        o_ref[:, cs] = within + carry
        carry = carry + jnp.sum(x, -1, keepdims=True)


def _suffix_bias(lf_rows, tail, *, tm):
    rows, past = lf_rows.shape
    return pl.pallas_call(
        _suffix_kernel,
        grid=(rows // tm,),
        in_specs=[pl.BlockSpec((tm, past), lambda i: (i, 0)), pl.BlockSpec((tm, 1), lambda i: (i, 0))],
        out_specs=pl.BlockSpec((tm, past), lambda i: (i, 0)),
        out_shape=jax.ShapeDtypeStruct((rows, past), F32),
        compiler_params=_params(1, 32),
        name="suffix_bias",
    )(lf_rows, tail)


def _sample_attn_kernel(pt_ref, q_ref, kn_ref, vn_ref, bp_ref, bn_ref, gm_ref, ck_hbm, cv_hbm, o_ref,
                        kbuf, vbuf, kb_sc, vb_sc, s_sc, p_sc, sem, *, layer, n_pages, page, dec_seq):
    n = pl.program_id(0)
    d_b = q_ref.shape[-1]
    n_heads = d_b // HEAD_W
    past = n_pages * page
    n_keys = past + LANES

    def page_copies(seq, slot):
        copies = []
        for p in range(n_pages):
            pg = pt_ref[seq, p]
            rows = pl.ds(p * page * n_heads, page * n_heads)
            copies.append(pltpu.make_async_copy(ck_hbm.at[layer, pg], kbuf.at[slot, rows], sem.at[0, slot]))
            copies.append(pltpu.make_async_copy(cv_hbm.at[layer, pg], vbuf.at[slot, rows], sem.at[1, slot]))
        return copies

    @pl.when(n == 0)
    def _():
        for c in page_copies(0, 0):
            c.start()

    @pl.when(n + 1 < pl.num_programs(0))
    def _():
        for c in page_copies(n + 1, (n + 1) % 2):
            c.start()

    slot = n % 2
    for c in page_copies(n, slot):
        c.wait()

    qrep = jnp.concatenate([q_ref[...]] * (LANES // dec_seq), axis=0)
    rowi = lax.broadcasted_iota(jnp.int32, (LANES, d_b), 0)
    lanei = lax.broadcasted_iota(jnp.int32, (LANES, d_b), 1)
    qbd = jnp.where(lanei // HEAD_W == rowi // dec_seq, qrep, 0.0).astype(BF16)

    pad = jnp.zeros((LANES - dec_seq, HEAD_W), F32)
    for h in range(n_heads):
        hs = slice(h * HEAD_W, (h + 1) * HEAD_W)
        head_rows = pl.ds(h, past, stride=n_heads)
        kb_sc[:past, hs] = kbuf.at[slot][head_rows, :].astype(BF16)
        vb_sc[:past, hs] = vbuf.at[slot][head_rows, :].astype(BF16)
        kb_sc[past:, hs] = jnp.concatenate([kn_ref[:, h, :], pad], axis=0).astype(BF16)
        vb_sc[past:, hs] = jnp.concatenate([vn_ref[:, h, :], pad], axis=0).astype(BF16)

    sc = lax.dot_general(kb_sc[...], qbd, (((1,), (1,)), ((), ())), preferred_element_type=F32) * ATTN_SCALE

    def expand(b):
        col = lax.broadcasted_iota(jnp.int32, (b.shape[0], LANES), 1)
        out = jnp.zeros((b.shape[0], LANES), F32)
        for h in range(n_heads):
            out = jnp.where(col // dec_seq == h, b[:, h:h + 1], out)
        return out

    bp = bp_ref[...]
    pieces = [jnp.broadcast_to(bp[h:h + 1, :], (dec_seq, past)) for h in range(n_heads)]
    if n_heads * dec_seq < LANES:
        pieces.append(jnp.zeros((LANES - n_heads * dec_seq, past), F32))
    b_rep = jnp.concatenate(pieces, axis=0)
    for c in range(past // LANES):
        cs = slice(c * LANES, (c + 1) * LANES)
        s_sc[cs, :] = sc[cs, :] + b_rep[:, cs].T
    krow = lax.broadcasted_iota(jnp.int32, (LANES, LANES), 0)
    qcol = lax.broadcasted_iota(jnp.int32, (LANES, LANES), 1) % dec_seq
    bn = jnp.concatenate([expand(bn_ref[...]), jnp.zeros((LANES - dec_seq, LANES), F32)], axis=0)
    s_sc[past:, :] = jnp.where(krow <= qcol, sc[past:, :] + bn, -jnp.inf)

    m = jnp.max(s_sc[...], axis=0, keepdims=True)
    l = jnp.zeros((LANES, 1), F32)
    for c in range(n_keys // LANES):
        cs = slice(c * LANES, (c + 1) * LANES)
        pt = jnp.exp(s_sc[cs, :] - m).T
        l = l + jnp.sum(pt, -1, keepdims=True)
        p_sc[:, cs] = pt.astype(BF16)
    acc = jnp.dot(p_sc[...], vb_sc[...], preferred_element_type=F32)
    for h in range(n_heads):
        rs = slice(h * dec_seq, (h + 1) * dec_seq)
        hs = slice(h * HEAD_W, (h + 1) * HEAD_W)
        o = acc[rs, hs] / l[rs, :]
        o_ref[:, hs] = _head_rms(o) * gm_ref[:, hs]


def _sample_attention(page_table, qs, k_new, v_new, bias_past, bias_new, gm_b, cache_k, cache_v, *,
                      layer, dec_seq):
    n_seq, n_pages = page_table.shape
    depth, n_pool, page, n_heads, head_w = cache_k.shape
    d_b = n_heads * head_w
    past = n_pages * page
    cache_k = cache_k.reshape(depth, n_pool, page * n_heads, head_w)
    cache_v = cache_v.reshape(depth, n_pool, page * n_heads, head_w)

    new_rows = pl.BlockSpec((None, dec_seq, n_heads, head_w), lambda n, pt: (layer, n, 0, 0))
    grid_spec = pltpu.PrefetchScalarGridSpec(
        num_scalar_prefetch=1,
        grid=(n_seq,),
        in_specs=[
            pl.BlockSpec((dec_seq, d_b), lambda n, pt: (n, 0)),
            new_rows,
            new_rows,
            pl.BlockSpec((None, n_heads, past), lambda n, pt: (n, 0, 0)),
            pl.BlockSpec((None, dec_seq, n_heads), lambda n, pt: (n, 0, 0)),
            pl.BlockSpec((1, d_b), lambda n, pt: (0, 0)),
            pl.BlockSpec(memory_space=pl.ANY),
            pl.BlockSpec(memory_space=pl.ANY),
        ],
        out_specs=pl.BlockSpec((dec_seq, d_b), lambda n, pt: (n, 0)),
        scratch_shapes=[
            pltpu.VMEM((2, past * n_heads, head_w), F32),
            pltpu.VMEM((2, past * n_heads, head_w), F32),
            pltpu.VMEM((past + LANES, d_b), BF16),
            pltpu.VMEM((past + LANES, d_b), BF16),
            pltpu.VMEM((past + LANES, LANES), F32),
            pltpu.VMEM((LANES, past + LANES), BF16),
            pltpu.SemaphoreType.DMA((2, 2)),
        ],
    )
    kern = functools.partial(_sample_attn_kernel, layer=layer, n_pages=n_pages, page=page, dec_seq=dec_seq)
    return pl.pallas_call(
        kern,
        grid_spec=grid_spec,
        out_shape=jax.ShapeDtypeStruct((n_seq * dec_seq, d_b), F32),
        compiler_params=_params(1, 56),
        name="sample_attn",
    )(page_table, qs, k_new, v_new, bias_past, bias_new, gm_b, cache_k, cache_v)


def _mix_kernel(ma_ref, mbp_ref, mbs_ref, mc_ref, x_ref, w_ref, g1_ref, b1_ref, wrh_ref, wrl_ref, br_ref,
                x1g_ref, gid_ref, mixed_sc, *, n_prompt_tiles, alpha):
    i = pl.program_id(0)
    tm, d = x_ref.shape
    d_a = ma_ref.shape[-1]
    d_b = mbp_ref.shape[-1]
    mixed_sc[:, :d_a] = ma_ref[...]
    mixed_sc[:, d_a + d_b:] = mc_ref[...]

    @pl.when(i < n_prompt_tiles)
    def _():
        mixed_sc[:, d_a:d_a + d_b] = mbp_ref[...]

    @pl.when(i >= n_prompt_tiles)
    def _():
        mixed_sc[:, d_a:d_a + d_b] = mbs_ref[...].astype(BF16)

    y = jnp.dot(mixed_sc[...], w_ref[...], preferred_element_type=F32)
    x1 = _layer_norm(alpha * x_ref[...] + y, g1_ref[...], b1_ref[...])
    x1g_ref[:, :d] = x1

    hi = x1.astype(BF16)
    lo = (x1 - hi.astype(F32)).astype(BF16)
    logits = (jnp.dot(hi, wrh_ref[...], preferred_element_type=F32)
              + jnp.dot(lo, wrh_ref[...], preferred_element_type=F32)
              + jnp.dot(hi, wrl_ref[...], preferred_element_type=F32))
    score_all = jax.nn.sigmoid(logits.T)
    sel_all = score_all + br_ref[...]
    score = [score_all[e:e + 1, :] for e in range(N_EXPERTS)]
    sel = [sel_all[e:e + 1, :] for e in range(N_EXPERTS)]

    gscore = []
    for g in range(N_GROUPS):
        a0, a1, a2, a3 = sel[4 * g:4 * g + 4]
        h1, l1 = jnp.maximum(a0, a1), jnp.minimum(a0, a1)
        h2, l2 = jnp.maximum(a2, a3), jnp.minimum(a2, a3)
        gscore.append(jnp.maximum(h1, h2) + jnp.maximum(jnp.minimum(h1, h2), jnp.maximum(l1, l2)))
    gmax = functools.reduce(jnp.maximum, gscore)
    gid = jnp.full(gscore[0].shape, N_GROUPS - 1, jnp.int32)
    for g in range(N_GROUPS - 2, -1, -1):
        gid = jnp.where(gscore[g] == gmax, g, gid)
    gid_ref[...] = gid

    chosen = []
    for e in range(N_EXPERTS):
        g = e // EXPERTS_PER_GROUP
        rank = jnp.zeros(sel[e].shape, jnp.int32)
        for o in range(4 * g, 4 * g + 4):
            if o < e:
                rank = rank + (sel[o] >= sel[e]).astype(jnp.int32)
            elif o > e:
                rank = rank + (sel[o] > sel[e]).astype(jnp.int32)
        chosen.append(jnp.logical_and(gid == g, rank < 2))
    wsum = functools.reduce(jnp.add, [jnp.where(chosen[e], score[e], 0.0) for e in range(N_EXPERTS)])
    rowi = lax.broadcasted_iota(jnp.int32, (N_EXPERTS, tm), 0)
    gates_t = jnp.zeros((N_EXPERTS, tm), F32)
    for e in range(N_EXPERTS):
        gates_t = jnp.where(rowi == e, jnp.where(chosen[e], score[e] / wsum, 0.0), gates_t)
    gates_t = jnp.concatenate([gates_t, jnp.zeros((LANES - N_EXPERTS, tm), F32)], axis=0)
    x1g_ref[:, d:] = gates_t.T


def _mix(ma, mbp, mbs, mc, x, w_out, g1, b1, wr_hi, wr_lo, br_col, *, tm, n_prompt_rows, alpha):
    r, d = x.shape
    d_a, d_b, d_c = ma.shape[1], mbp.shape[1], mc.shape[1]
    npt = n_prompt_rows // tm
    rows = lambda i: (i, 0)
    return pl.pallas_call(
        functools.partial(_mix_kernel, n_prompt_tiles=npt, alpha=alpha),
        grid=(r // tm,),
        in_specs=[
            pl.BlockSpec((tm, d_a), rows),
            pl.BlockSpec((tm, d_b), lambda i: (jnp.minimum(i, npt - 1), 0)),
            pl.BlockSpec((tm, d_b), lambda i: (jnp.maximum(i - npt, 0), 0)),
            pl.BlockSpec((tm, d_c), rows),
            pl.BlockSpec((tm, d), rows),
            _resident((d_a + d_b + d_c, d)),
            _resident((1, d)),
            _resident((1, d)),
            _resident((d, LANES)),
            _resident((d, LANES)),
            _resident((LANES, 1)),
        ],
        out_specs=[
            pl.BlockSpec((tm, d + LANES), rows),
            pl.BlockSpec((1, tm), lambda i: (0, i)),
        ],
        out_shape=[
            jax.ShapeDtypeStruct((r, d + LANES), F32),
            jax.ShapeDtypeStruct((1, r), jnp.int32),
        ],
        scratch_shapes=[pltpu.VMEM((tm, d_a + d_b + d_c), BF16)],
        compiler_params=_params(1, 40),
        name="mix",
    )(ma, mbp, mbs, mc, x, w_out, g1, b1, wr_hi, wr_lo, br_col)


def _moe_kernel(tok_ref, tg_ref, tr_ref, x1g_hbm, wg_ref, wu_ref, wd_ref, g2_ref, b2_ref, out_hbm,
                xg_sc, xb_sc, acc_sc, y_sc, sem, *, tm, d, alpha):
    t = pl.program_id(0)
    j = pl.program_id(1)
    n_real = tr_ref[t]
    valid = n_real > 0
    slot = t % 2
    quarter = tm // EXPERTS_PER_GROUP

    def row_in(tile, r, sl):
        return pltpu.make_async_copy(x1g_hbm.at[pl.ds(tok_ref[tile * tm + r], 1), :],
                                     xg_sc.at[sl, pl.ds(r, 1), :], sem.at[sl])

    def wait_rows(sl):
        pltpu.make_async_copy(x1g_hbm.at[pl.ds(0, tm), :], xg_sc.at[sl], sem.at[sl]).wait()

    def row_out(r):
        return pltpu.make_async_copy(y_sc.at[pl.ds(r, 1), :], out_hbm.at[pl.ds(tok_ref[t * tm + r], 1), :],
                                     sem.at[2])

    def for_rows(n, fn, unroll):
        def body(r, carry):
            fn(r)
            return carry
        lax.fori_loop(0, n, body, 0, unroll=unroll)

    @pl.when(jnp.logical_and(valid, jnp.logical_and(j == 0, t == 0)))
    def _():
        for_rows(tm, lambda r: row_in(0, r, 0).start(), ROW_DMA_UNROLL)

    @pl.when(jnp.logical_and(valid, j == 0))
    def _():
        wait_rows(slot)
        xb_sc[...] = xg_sc[slot, :, :d].astype(BF16)
        acc_sc[...] = jnp.zeros_like(acc_sc)

    @pl.when(valid)
    def _():
        for k in range(quarter):
            row_in(t + 1, j * quarter + k, 1 - slot).start()
        xb = xb_sc[...]
        hid = (jax.nn.silu(jnp.dot(xb, wg_ref[...].astype(BF16), preferred_element_type=F32))
               * jnp.dot(xb, wu_ref[...].astype(BF16), preferred_element_type=F32))
        lane = lax.broadcasted_iota(jnp.int32, (tm, LANES), 1)
        expert = tg_ref[t] * EXPERTS_PER_GROUP + j
        gate = jnp.sum(jnp.where(lane == expert, xg_sc[slot, :, d:], 0.0), -1, keepdims=True)
        acc_sc[...] += jnp.dot((hid * gate).astype(BF16), wd_ref[...].astype(BF16), preferred_element_type=F32)

    @pl.when(jnp.logical_and(valid, j == EXPERTS_PER_GROUP - 1))
    def _():
        y_sc[...] = _layer_norm(alpha * xg_sc[slot, :, :d] + acc_sc[...], g2_ref[...], b2_ref[...])
        for_rows(n_real, lambda r: row_out(r).start(), 1)
        for_rows(n_real, lambda r: row_out(r).wait(), 1)

        @pl.when(tr_ref[t + 1] == 0)
        def _():
            wait_rows(1 - slot)


def _moe(token, tile_group, tile_rows, x1g, wg, wu, wd, g2, b2, *, layer, tm, n_rows, alpha):
    d = x1g.shape[1] - LANES
    d_ff = wg.shape[-1]
    n_tiles = tile_group.shape[0]

    def expert(t, j, tok, tg, tr):
        return layer, tg[t] * EXPERTS_PER_GROUP + jnp.where(tr[t] > 0, j, EXPERTS_PER_GROUP - 1), 0, 0

    grid_spec = pltpu.PrefetchScalarGridSpec(
        num_scalar_prefetch=3,
        grid=(n_tiles, EXPERTS_PER_GROUP),
        in_specs=[
            pl.BlockSpec(memory_space=pl.ANY),
            pl.BlockSpec((None, None, d, d_ff), expert),
            pl.BlockSpec((None, None, d, d_ff), expert),
            pl.BlockSpec((None, None, d_ff, d), expert),
            pl.BlockSpec((1, d), lambda *a: (0, 0)),
            pl.BlockSpec((1, d), lambda *a: (0, 0)),
        ],
        out_specs=pl.BlockSpec(memory_space=pl.ANY),
        scratch_shapes=[
            pltpu.VMEM((2, tm, d + LANES), F32),
            pltpu.VMEM((tm, d), BF16),
            pltpu.VMEM((tm, d), F32),
            pltpu.VMEM((tm, d), F32),
            pltpu.SemaphoreType.DMA((3,)),
        ],
    )
    return pl.pallas_call(
        functools.partial(_moe_kernel, tm=tm, d=d, alpha=alpha),
        grid_spec=grid_spec,
        out_shape=jax.ShapeDtypeStruct((n_rows, d), F32),
        compiler_params=_params(2, 56),
        name="moe",
    )(token, tile_group, tile_rows, x1g, wg, wu, wd, g2, b2)


def _sort_by_group(gid, *, tm, n_tiles):
    r = gid.shape[0]
    onehot = (gid[:, None] == jnp.arange(N_GROUPS, dtype=jnp.int32)[None, :]).astype(jnp.int32)
    rank = jnp.sum((jnp.cumsum(onehot, axis=0) - onehot) * onehot, axis=1)
    counts = jnp.sum(onehot, axis=0)
    tiles = (counts + tm - 1) // tm
    tile_end = jnp.cumsum(tiles)
    tile_start = tile_end - tiles
    slot = tile_start[gid] * tm + rank
    token = jnp.zeros(((n_tiles + 1) * tm,), jnp.int32).at[slot].set(jnp.arange(r, dtype=jnp.int32))
    tile_ids = jnp.arange(n_tiles, dtype=jnp.int32)
    tile_group = jnp.minimum(jnp.sum(tile_ids[:, None] >= tile_end[None, :], axis=1), N_GROUPS - 1).astype(jnp.int32)
    tile_rows = jnp.clip(counts[tile_group] - (tile_ids - tile_start[tile_group]) * tm, 0, tm)
    tile_rows = jnp.where(tile_ids < tile_end[-1], tile_rows, 0).astype(jnp.int32)
    last_group = tile_group[jnp.maximum(tile_end[-1] - 1, 0)]
    tile_group = jnp.where(tile_rows > 0, tile_group, last_group)
    return token, tile_group, jnp.pad(tile_rows, (0, 1))


def _embed_kernel(x_ref, p_ref, wg_ref, bg_ref, wp_ref, *o_refs, n_prompt_tiles):
    x = x_ref[...]
    gate = jax.nn.sigmoid(jnp.dot(x.astype(BF16), wg_ref[...], preferred_element_type=F32) + bg_ref[...])
    proj = jnp.dot(p_ref[...].astype(BF16), wp_ref[...], preferred_element_type=F32)
    y = x + gate * proj
    if len(o_refs) == 1:
        o_refs[0][...] = y
    else:
        @pl.when(pl.program_id(0) < n_prompt_tiles)
        def _():
            o_refs[0][...] = y

        @pl.when(pl.program_id(0) >= n_prompt_tiles)
        def _():
            o_refs[1][...] = y


def _embed(x2, p, w_pg, b_pg, w_pp, *, tm, n_prompt_rows, split):
    n_rows, d = x2.shape
    d_p = p.shape[1]
    npt = n_prompt_rows // tm
    rows = lambda i: (i, 0)
    if split:
        out_specs = [pl.BlockSpec((tm, d), lambda i: (jnp.minimum(i, npt - 1), 0)),
                     pl.BlockSpec((tm, d), lambda i: (jnp.maximum(i - npt, 0), 0))]
        out_shape = [jax.ShapeDtypeStruct((n_prompt_rows, d), F32),
                     jax.ShapeDtypeStruct((n_rows - n_prompt_rows, d), F32)]
    else:
        out_specs = [pl.BlockSpec((tm, d), rows)]
        out_shape = [jax.ShapeDtypeStruct((n_rows, d), F32)]
    return pl.pallas_call(
        functools.partial(_embed_kernel, n_prompt_tiles=npt),
        grid=(n_rows // tm,),
        in_specs=[
            pl.BlockSpec((tm, d), rows),
            pl.BlockSpec((tm, d_p), rows),
            _resident((d, d)),
            _resident((1, d)),
            _resident((d_p, d)),
        ],
        out_specs=out_specs,
        out_shape=out_shape,
        compiler_params=_params(1, 40),
        name="embed",
    )(x2, p, w_pg, b_pg, w_pp)


def kernel(x_prompt, x_sample, p_prompt, p_sample, cache_k, cache_v, cache_logf, state_conv, page_table, w_in, b_f, g_v, w_s, b_s, conv_w, g_mix, w_out, ln1_g, ln1_b, w_router, b_router, w_gate, w_up, w_down, ln2_g, ln2_b, w_ple_gate, b_ple_gate, w_ple_proj):
    depth = w_in.shape[0]
    batch, seq, d = x_prompt.shape
    n_seq, dec_seq, _ = x_sample.shape
    n_heads = cache_k.shape[3]
    d_a = g_v.shape[-1]
    d_b = n_heads * HEAD_W
    d_c = conv_w.shape[-1]
    np_rows = batch * seq
    ns_rows = n_seq * dec_seq
    n_rows = np_rows + ns_rows
    alpha = (2 * depth) ** 0.25
    tm = 256
    tm_moe = 512
    tq = 512
    assert batch == 1 and CONV_W - 1 <= dec_seq <= CHUNK and dec_seq & (dec_seq - 1) == 0
    assert np_rows % tq == 0 and ns_rows % tm == 0 and np_rows % tm_moe == 0 and ns_rows % tm_moe == 0
    assert w_s.shape[-1] == CHUNK and cache_k.shape[-1] == HEAD_W and n_heads <= SUBLANES
    assert (n_seq * n_heads) % tm == 0 and (np_rows // tq) % 2 == 0
    assert w_gate.shape[1] == N_EXPERTS and w_in.shape[-1] == 2 * d_a + 3 * d_b + n_heads + 3 * d_c
    n_tiles = n_rows // tm_moe + N_GROUPS

    x = jnp.concatenate([x_prompt.reshape(np_rows, d), x_sample.reshape(ns_rows, d)], axis=0)
    wr_pad = jnp.pad(w_router, ((0, 0), (0, LANES - N_EXPERTS)))
    wr_hi = wr_pad.astype(BF16)
    wr_lo = (wr_pad - wr_hi.astype(F32)).astype(BF16)
    br_col = jnp.pad(b_router, (0, LANES - N_EXPERTS)).reshape(LANES, 1)
    reps = CHUNK // dec_seq

    k_prompt = jnp.zeros((depth, np_rows, n_heads, HEAD_W), F32)
    v_prompt = jnp.zeros((depth, np_rows, n_heads, HEAD_W), F32)
    k_sample = jnp.zeros((depth, ns_rows, n_heads, HEAD_W), F32)
    v_sample = jnp.zeros((depth, ns_rows, n_heads, HEAD_W), F32)
    outs = [[] for _ in range(5)]
    for i in range(depth):
        o_q, o_k, o_v = 2 * d_a, 2 * d_a + d_b, 2 * d_a + 2 * d_b
        o_f = 2 * d_a + 3 * d_b
        o_c = o_f + n_heads
        wi = w_in[i]
        w_a = wi[:, :o_q].astype(BF16)
        w_q = wi[:, o_q:o_k].astype(BF16)
        w_k = wi[:, o_k:o_v].astype(BF16)
        w_v = wi[:, o_v:o_f].astype(BF16)
        w_c = jnp.concatenate([wi[:, o_c:], jnp.pad(wi[:, o_f:o_c], ((0, 0), (0, LANES - n_heads)))], axis=1).astype(BF16)
        bf_pad = jnp.pad(b_f[i], (0, LANES - n_heads)).reshape(1, LANES)
        gm = g_mix[i].reshape(1, -1)
        gm_a, gm_b, gm_c = gm[:, :d_a], gm[:, d_a:d_a + d_b], gm[:, d_a + d_b:]
        ws_all = jnp.stack([w_s[i], jnp.tile(w_s[i][:, :dec_seq, :dec_seq], (1, reps, reps))])
        bs_rows = lambda b: jnp.repeat(b.T, HEAD_W, axis=1)
        bs_all = jnp.stack([bs_rows(b_s[i]), bs_rows(jnp.tile(b_s[i][:, :dec_seq], (1, reps)))])
        st = state_conv[i]
        zero_rows = jnp.zeros((n_seq, dec_seq - 1, d_c), F32)
        s1 = jnp.concatenate([st[:, 1:2], zero_rows], axis=1).reshape(ns_rows, d_c)
        s2 = jnp.concatenate([st[:, 0:1], st[:, 1:2], zero_rows[:, 1:]], axis=1).reshape(ns_rows, d_c)

        av_s, ma = _branch_a(x, w_a, g_v[i].reshape(1, d_a), ws_all, bs_all, gm_a,
                             tm=tm, n_prompt_rows=np_rows, dec_seq=dec_seq)
        qt, qs = _project(x, w_q, tm=tq, n_prompt_rows=np_rows, t_scale=ATTN_SCALE * LOG2E, sample_rows=True)
        kb, k_prompt, k_sample = _project(x, w_k, tm=tq, n_prompt_rows=np_rows, flat=True,
                                          cache=(k_prompt, k_sample, i))
        vt, v_prompt, v_sample = _project(x, w_v, tm=tq, n_prompt_rows=np_rows, t_scale=1.0,
                                          cache=(v_prompt, v_sample, i))
        zc, mc, lf, kbias = _branch_c(x, w_c, bf_pad, conv_w[i], gm_c, s1, s2,
                                      tm=tm, n_prompt_rows=np_rows, dec_seq=dec_seq, d_b=d_b)
        mbp = _prompt_attention(qt, kb, kbias, vt, gm_b, n_prompt_rows=np_rows, tq=tq, n_sub=2)

        lf_new = lf[np_rows:, :n_heads].reshape(n_seq, dec_seq, n_heads)
        pages_t = jnp.swapaxes(cache_logf[i], 1, 2)[page_table]
        lf_past_t = jnp.swapaxes(pages_t, 1, 2).reshape(n_seq, n_heads, -1)
        past = lf_past_t.shape[2]
        new_total = jnp.sum(lf_new, axis=1)
        bias_new = new_total[:, None, :] - jnp.cumsum(lf_new, axis=1)
        bias_past = _suffix_bias(lf_past_t.reshape(n_seq * n_heads, past), new_total.reshape(-1, 1), tm=tm)
        mbs = _sample_attention(page_table, qs, k_sample, v_sample, bias_past.reshape(n_seq, n_heads, past),
                                bias_new, gm_b, cache_k, cache_v, layer=i, dec_seq=dec_seq)

        x1g, gid = _mix(ma, mbp, mbs, mc, x, w_out[i].astype(BF16), ln1_g[i].reshape(1, d), ln1_b[i].reshape(1, d),
                        wr_hi, wr_lo, br_col, tm=tm, n_prompt_rows=np_rows, alpha=alpha)
        token, tile_group, tile_rows = _sort_by_group(gid[0], tm=tm_moe, n_tiles=n_tiles)
        x2 = _moe(token, tile_group, tile_rows, x1g, w_gate, w_up, w_down,
                  ln2_g[i].reshape(1, d), ln2_b[i].reshape(1, d),
                  layer=i, tm=tm_moe, n_rows=n_rows, alpha=alpha)
        p = jnp.concatenate([p_prompt[i].reshape(np_rows, -1), p_sample[i].reshape(ns_rows, -1)], axis=0)
        x = _embed(x2, p, w_ple_gate[i].astype(BF16), b_ple_gate[i].reshape(1, d), w_ple_proj[i].astype(BF16),
                   tm=tm, n_prompt_rows=np_rows, split=i == depth - 1)
        x = x[0] if i < depth - 1 else x

        outs[0].append(lf[:np_rows, :n_heads].reshape(batch, seq, n_heads))
        outs[1].append(zc[np_rows - (CONV_W - 1):np_rows].reshape(batch, CONV_W - 1, d_c))
        outs[2].append(av_s.reshape(n_seq, dec_seq, d_a))
        outs[3].append(lf_new)
        outs[4].append(zc[np_rows:].reshape(n_seq, dec_seq, d_c)[:, dec_seq - (CONV_W - 1):])

    new_lf_p, new_conv_p, new_av_s, new_lf_s, new_conv_s = (jnp.stack(o) for o in outs)
    return (x[0].reshape(batch, seq, d), x[1].reshape(n_seq, dec_seq, d),
            k_prompt.reshape(depth, batch, seq, n_heads, HEAD_W), v_prompt.reshape(depth, batch, seq, n_heads, HEAD_W),
            new_lf_p, new_conv_p, new_av_s,
            k_sample.reshape(depth, n_seq, dec_seq, n_heads, HEAD_W),
            v_sample.reshape(depth, n_seq, dec_seq, n_heads, HEAD_W),
            new_lf_s, new_conv_s)
```

```python
import functools

import numpy as np
import jax
import jax.numpy as jnp
from jax import lax
from jax.experimental import pallas as pl
from jax.experimental.pallas import tpu as pltpu

F32 = jnp.float32
BF16 = jnp.bfloat16

HEAD_W = 128
CHUNK = 128
CONV_W = 3
N_GROUPS = 4
EXPERTS_PER_GROUP = 4
N_EXPERTS = N_GROUPS * EXPERTS_PER_GROUP
LN_EPS = 1e-5
RMS_EPS = 1e-6
ATTN_SCALE = HEAD_W ** -0.5
LOG2E = float(np.log2(np.e))
LANES = 128
SUBLANES = 8
MIB = 1024 * 1024
ROW_DMA_UNROLL = 8


def _params(n_axes, vmem_mib):
    return pltpu.CompilerParams(dimension_semantics=("arbitrary",) * n_axes,
                                vmem_limit_bytes=vmem_mib * MIB)


def _resident(shape):
    zeros = (0,) * len(shape)
    return pl.BlockSpec(shape, lambda *_: zeros, pipeline_mode=pl.Buffered(1))


def _layer_norm(v, gain, bias=None):
    mu = jnp.mean(v, -1, keepdims=True)
    var = jnp.mean(jnp.square(v - mu), -1, keepdims=True)
    y = (v - mu) * lax.rsqrt(var + LN_EPS) * gain
    return y if bias is None else y + bias


def _head_rms(o):
    return o * lax.rsqrt(jnp.mean(o * o, -1, keepdims=True) + RMS_EPS)


def _split3(v):
    hi = v.astype(BF16)
    r1 = v - hi.astype(F32)
    mid = r1.astype(BF16)
    lo = (r1 - mid.astype(F32)).astype(BF16)
    return hi, mid, lo


def _proj_kernel(*refs, n_prompt_tiles, flat, t_scale, sample_rows, cache):
    refs = iter(refs)
    x_ref, w_ref = next(refs), next(refs)
    if cache:
        next(refs), next(refs)
    i = pl.program_id(0)
    z = jnp.dot(x_ref[...], w_ref[...], preferred_element_type=F32)
    if flat:
        next(refs)[...] = z.astype(BF16)
    if t_scale is not None:
        next(refs)[...] = (z * t_scale).T.astype(BF16)
    if sample_rows:
        next(refs)[...] = z
    if cache:
        prompt_ref, sample_ref = next(refs), next(refs)

        def heads_to(o_ref):
            for h in range(o_ref.shape[1]):
                o_ref[:, h, :] = z[:, h * HEAD_W:(h + 1) * HEAD_W]

        @pl.when(i < n_prompt_tiles)
        def _():
            heads_to(prompt_ref)

        @pl.when(i >= n_prompt_tiles)
        def _():
            heads_to(sample_ref)


def _project(x, w, *, tm, n_prompt_rows, flat=False, t_scale=None, sample_rows=False, cache=None):
    r, d = x.shape
    n = w.shape[1]
    npt = n_prompt_rows // tm
    srows = lambda i: (jnp.maximum(i - npt, 0), 0)
    operands = [x, w]
    in_specs = [pl.BlockSpec((tm, d), lambda i: (i, 0)), _resident((d, n))]
    out_shape, out_specs, aliases = [], [], {}
    if flat:
        out_shape.append(jax.ShapeDtypeStruct((r, n), BF16))
        out_specs.append(pl.BlockSpec((tm, n), lambda i: (i, 0)))
    if t_scale is not None:
        out_shape.append(jax.ShapeDtypeStruct((r // tm, n, tm), BF16))
        out_specs.append(pl.BlockSpec((None, n, tm), lambda i: (i, 0, 0)))
    if sample_rows:
        out_shape.append(jax.ShapeDtypeStruct((r - n_prompt_rows, n), F32))
        out_specs.append(pl.BlockSpec((tm, n), srows))
    if cache is not None:
        prompt_buf, sample_buf, layer = cache
        heads = n // HEAD_W
        for buf, tile in ((prompt_buf, lambda i: jnp.minimum(i, npt - 1)), (sample_buf, lambda i: jnp.maximum(i - npt, 0))):
            aliases[len(operands)] = len(out_shape)
            operands.append(buf)
            in_specs.append(pl.BlockSpec(memory_space=pl.ANY))
            out_shape.append(jax.ShapeDtypeStruct(buf.shape, buf.dtype))
            out_specs.append(pl.BlockSpec((None, tm, heads, HEAD_W), functools.partial(
                lambda i, tile: (layer, tile(i), 0, 0), tile=tile)))
    kern = functools.partial(_proj_kernel, n_prompt_tiles=npt, flat=flat, t_scale=t_scale,
                             sample_rows=sample_rows, cache=cache is not None)
    return pl.pallas_call(
        kern,
        grid=(r // tm,),
        in_specs=in_specs,
        out_specs=out_specs,
        out_shape=out_shape,
        input_output_aliases=aliases,
        compiler_params=_params(1, 40),
        name="proj",
    )(*operands)


def _branch_a_kernel(x_ref, w_ref, gv_ref, ws_ref, bs_ref, gm_ref, av_ref, ma_ref, *,
                     n_prompt_tiles, tm, dec_shift):
    is_sample = pl.program_id(0) >= n_prompt_tiles
    d_a = gv_ref.shape[-1]
    z = jnp.dot(x_ref[...], w_ref[...], preferred_element_type=F32)
    u = jax.nn.gelu(z[:, :d_a])
    av = _layer_norm(jax.nn.gelu(z[:, d_a:]), gv_ref[...])
    av_ref[...] = av
    avb = av.astype(BF16)
    row = lax.broadcasted_iota(jnp.int32, (CHUNK, CHUNK), 0)
    col = lax.broadcasted_iota(jnp.int32, (CHUNK, CHUNK), 1)
    shift = jnp.where(is_sample, dec_shift, int(np.log2(CHUNK)))
    mask = jnp.logical_and(col <= row, jnp.right_shift(row, shift) == jnp.right_shift(col, shift))
    for h in range(d_a // HEAD_W):
        hs = slice(h * HEAD_W, (h + 1) * HEAD_W)
        wm = jnp.where(mask, ws_ref[h], 0.0).astype(BF16)
        for c in range(tm // CHUNK):
            cs = slice(c * CHUNK, (c + 1) * CHUNK)
            mix = jnp.dot(wm, avb[cs, hs], preferred_element_type=F32) + bs_ref[:, hs]
            o = _head_rms(u[cs, hs] * mix) * gm_ref[:, hs]
            ma_ref[cs, hs] = o.astype(BF16)


def _branch_a(x, w_a, g_v, ws_all, bs_all, gm_a, *, tm, n_prompt_rows, dec_seq):
    r, d = x.shape
    d_a = g_v.shape[-1]
    npt = n_prompt_rows // tm
    n_heads = d_a // HEAD_W
    kern = functools.partial(_branch_a_kernel, n_prompt_tiles=npt, tm=tm, dec_shift=int(np.log2(dec_seq)))
    sel = lambda i: (i >= npt).astype(jnp.int32)
    return pl.pallas_call(
        kern,
        grid=(r // tm,),
        in_specs=[
            pl.BlockSpec((tm, d), lambda i: (i, 0)),
            _resident((d, 2 * d_a)),
            _resident((1, d_a)),
            pl.BlockSpec((None, n_heads, CHUNK, CHUNK), lambda i: (sel(i), 0, 0, 0)),
            pl.BlockSpec((None, CHUNK, d_a), lambda i: (sel(i), 0, 0)),
            _resident((1, d_a)),
        ],
        out_specs=[
            pl.BlockSpec((tm, d_a), lambda i: (jnp.maximum(i - npt, 0), 0)),
            pl.BlockSpec((tm, d_a), lambda i: (i, 0)),
        ],
        out_shape=[
            jax.ShapeDtypeStruct((r - n_prompt_rows, d_a), F32),
            jax.ShapeDtypeStruct((r, d_a), BF16),
        ],
        compiler_params=_params(1, 32),
        name="branch_a",
    )(x, w_a, g_v, ws_all, bs_all, gm_a)


def _branch_c_kernel(x_ref, w_ref, bf_ref, cw_ref, gm_ref, s1_ref, s2_ref,
                     zc_ref, mc_ref, lf_ref, kb_ref, carry_z, carry_c, *,
                     n_prompt_tiles, tm, dec_seq):
    i = pl.program_id(0)
    d_c = gm_ref.shape[-1]

    @pl.when(i == 0)
    def _():
        carry_z[...] = jnp.zeros_like(carry_z)
        carry_c[...] = jnp.zeros_like(carry_c)

    z = jnp.dot(x_ref[...], w_ref[...], preferred_element_type=F32)
    c_b = z[:, :d_c]
    zc = z[:, d_c:2 * d_c] * z[:, 2 * d_c:3 * d_c]
    zc_ref[...] = zc

    lf = jax.nn.log_sigmoid(z[:, 3 * d_c:] + bf_ref[...])
    lf_ref[...] = lf
    trow = lax.broadcasted_iota(jnp.int32, (tm, tm), 0)
    tcol = lax.broadcasted_iota(jnp.int32, (tm, tm), 1)
    tri = jnp.where(tcol <= trow, 1.0, 0.0).astype(BF16)
    hi, mid, lo = _split3(lf)
    cs = (jnp.dot(tri, hi, preferred_element_type=F32) + jnp.dot(tri, mid, preferred_element_type=F32)
          + jnp.dot(tri, lo, preferred_element_type=F32)) + carry_c[...]
    carry_c[...] = cs[tm - 1:tm, :]
    neg = cs * (-LOG2E)
    lane = lax.broadcasted_iota(jnp.int32, (tm, HEAD_W), 1)
    for h in range(kb_ref.shape[-1] // HEAD_W):
        hi, mid, lo = (t.astype(F32) for t in _split3(jnp.broadcast_to(neg[:, h:h + 1], (tm, HEAD_W))))
        terms = jnp.where(lane == 0, hi, jnp.where(lane == 1, mid, jnp.where(lane == 2, lo, 0.0)))
        kb_ref[:, h * HEAD_W:(h + 1) * HEAD_W] = terms.astype(BF16)

    row = lax.broadcasted_iota(jnp.int32, (tm, d_c), 0)
    back1 = pltpu.roll(zc, 1, 0)
    back2 = pltpu.roll(zc, 2, 0)

    def finish(zm1, zm2):
        y = cw_ref[0:1, :] * zm2 + cw_ref[1:2, :] * zm1 + cw_ref[2:3, :] * zc
        o = c_b * y
        for g in range(d_c // HEAD_W):
            gs = slice(g * HEAD_W, (g + 1) * HEAD_W)
            mc_ref[:, gs] = (_head_rms(o[:, gs]) * gm_ref[:, gs]).astype(BF16)

    @pl.when(i < n_prompt_tiles)
    def _():
        p6 = carry_z[SUBLANES - 2:SUBLANES - 1, :]
        p7 = carry_z[SUBLANES - 1:SUBLANES, :]
        finish(jnp.where(row >= 1, back1, p7),
               jnp.where(row >= 2, back2, jnp.where(row == 0, p6, p7)))

    @pl.when(i >= n_prompt_tiles)
    def _():
        t = jnp.bitwise_and(row, dec_seq - 1)
        finish(jnp.where(t >= 1, back1, s1_ref[...]), jnp.where(t >= 2, back2, s2_ref[...]))

    carry_z[...] = zc[tm - SUBLANES:tm, :]


def _branch_c(x, w_c, bf_pad, conv_w, gm_c, s1, s2, *, tm, n_prompt_rows, dec_seq, d_b):
    r, d = x.shape
    d_c = gm_c.shape[-1]
    n = w_c.shape[1]
    npt = n_prompt_rows // tm
    kern = functools.partial(_branch_c_kernel, n_prompt_tiles=npt, tm=tm, dec_seq=dec_seq)
    rows = lambda i: (i, 0)
    srows = lambda i: (jnp.maximum(i - npt, 0), 0)
    return pl.pallas_call(
        kern,
        grid=(r // tm,),
        in_specs=[
            pl.BlockSpec((tm, d), rows),
            _resident((d, n)),
            _resident((1, LANES)),
            _resident((CONV_W, d_c)),
            _resident((1, d_c)),
            pl.BlockSpec((tm, d_c), srows),
            pl.BlockSpec((tm, d_c), srows),
        ],
        out_specs=[
            pl.BlockSpec((tm, d_c), rows),
            pl.BlockSpec((tm, d_c), rows),
            pl.BlockSpec((tm, LANES), rows),
            pl.BlockSpec((tm, d_b), rows),
        ],
        out_shape=[
            jax.ShapeDtypeStruct((r, d_c), F32),
            jax.ShapeDtypeStruct((r, d_c), BF16),
            jax.ShapeDtypeStruct((r, LANES), F32),
            jax.ShapeDtypeStruct((r, d_b), BF16),
        ],
        scratch_shapes=[pltpu.VMEM((SUBLANES, d_c), F32), pltpu.VMEM((1, LANES), F32)],
        compiler_params=_params(1, 40),
        name="branch_c",
    )(x, w_c, bf_pad, conv_w, gm_c, s1, s2)


def _prompt_attn_kernel(qt_ref, k_ref, kb_ref, vt_ref, gm_ref, o_ref, m_sc, l_sc, acc_sc, *, tq):
    qi = pl.program_id(1)
    n_sub = qt_ref.shape[0]
    sub = lax.broadcasted_iota(jnp.int32, (HEAD_W, tq), 0)
    ones = jnp.where(sub < 3, 1.0, 0.0).astype(BF16)
    q_aug = [jnp.concatenate([qt_ref[s], ones], axis=0) for s in range(n_sub)]
    m_sc[...] = jnp.full_like(m_sc, -jnp.inf)
    l_sc[...] = jnp.zeros_like(l_sc)
    acc_sc[...] = jnp.zeros_like(acc_sc)

    def block(j, s, diagonal):
        k_aug = jnp.concatenate([k_ref[j], kb_ref[j]], axis=1)
        st = jnp.dot(k_aug, q_aug[s], preferred_element_type=F32)
        if diagonal:
            key = lax.broadcasted_iota(jnp.int32, (tq, tq), 0)
            query = lax.broadcasted_iota(jnp.int32, (tq, tq), 1)
            st = jnp.where(key <= query, st, -jnp.inf)
        m_old = m_sc[s]
        m_new = jnp.maximum(m_old, jnp.max(st, axis=0, keepdims=True))
        alpha = jnp.exp2(m_old - m_new)
        pt = jnp.exp2(st - m_new)
        l_sc[s] = alpha * l_sc[s] + jnp.sum(pt, axis=0, keepdims=True)
        acc_sc[s] = alpha * acc_sc[s] + jnp.dot(vt_ref[j], pt.astype(BF16), preferred_element_type=F32)
        m_sc[s] = m_new

    def body(j, carry):
        for s in range(n_sub):
            block(j, s, False)
        return carry

    lax.fori_loop(0, n_sub * qi, body, 0)
    for s in range(n_sub):
        for e in range(s + 1):
            block(n_sub * qi + e, s, e == s)
        o = (acc_sc[s] / l_sc[s]).T
        o_ref[s * tq:(s + 1) * tq, :] = (_head_rms(o) * gm_ref[...]).astype(o_ref.dtype)


def _prompt_attention(qt, kb, kbias, vt, gm_b, *, n_prompt_rows, tq, n_sub):
    d_b = kb.shape[1]
    n_heads = d_b // HEAD_W
    nk = n_prompt_rows // tq
    kb3 = kb.reshape(-1, tq, d_b)
    kbias3 = kbias.reshape(-1, tq, d_b)
    key_rows = pl.BlockSpec((nk, tq, HEAD_W), lambda h, qi: (0, 0, h))
    return pl.pallas_call(
        functools.partial(_prompt_attn_kernel, tq=tq),
        grid=(n_heads, nk // n_sub),
        in_specs=[
            pl.BlockSpec((n_sub, HEAD_W, tq), lambda h, qi: (qi, h, 0)),
            key_rows,
            key_rows,
            pl.BlockSpec((nk, HEAD_W, tq), lambda h, qi: (0, h, 0)),
            pl.BlockSpec((1, HEAD_W), lambda h, qi: (0, h)),
        ],
        out_specs=pl.BlockSpec((n_sub * tq, HEAD_W), lambda h, qi: (qi, h)),
        out_shape=jax.ShapeDtypeStruct((n_prompt_rows, d_b), BF16),
        scratch_shapes=[pltpu.VMEM((n_sub, 1, tq), F32), pltpu.VMEM((n_sub, 1, tq), F32),
                        pltpu.VMEM((n_sub, HEAD_W, tq), F32)],
        compiler_params=_params(2, 32),
        name="prompt_attn",
    )(qt, kb3, kbias3, vt, gm_b)


def _suffix_kernel(lf_ref, tail_ref, o_ref):
    past = lf_ref.shape[1]
    u = lax.broadcasted_iota(jnp.int32, (LANES, LANES), 0)
    s = lax.broadcasted_iota(jnp.int32, (LANES, LANES), 1)
    later = jnp.where(u > s, 1.0, 0.0).astype(BF16)
    carry = tail_ref[...]
    for c in reversed(range(past // LANES)):
        cs = slice(c * LANES, (c + 1) * LANES)
        x = lf_ref[:, cs]
        hi, mid, lo = _split3(x)
        within = (jnp.dot(hi, later, preferred_element_type=F32) + jnp.dot(mid, later, preferred_element_type=F32)
                  + jnp.dot(lo, later, preferred_element_type=F32))
        o_ref[:, cs] = within + carry
        carry = carry + jnp.sum(x, -1, keepdims=True)


def _suffix_bias(lf_rows, tail, *, tm):
    rows, past = lf_rows.shape
    return pl.pallas_call(
        _suffix_kernel,
        grid=(rows // tm,),
        in_specs=[pl.BlockSpec((tm, past), lambda i: (i, 0)), pl.BlockSpec((tm, 1), lambda i: (i, 0))],
        out_specs=pl.BlockSpec((tm, past), lambda i: (i, 0)),
        out_shape=jax.ShapeDtypeStruct((rows, past), F32),
        compiler_params=_params(1, 32),
        name="suffix_bias",
    )(lf_rows, tail)


def _sample_attn_kernel(pt_ref, q_ref, kn_ref, vn_ref, bp_ref, bn_ref, gm_ref, ck_hbm, cv_hbm, o_ref,
                        kbuf, vbuf, kb_sc, vb_sc, s_sc, p_sc, sem, *, layer, n_pages, page, dec_seq):
    n = pl.program_id(0)
    d_b = q_ref.shape[-1]
    n_heads = d_b // HEAD_W
    past = n_pages * page
    n_keys = past + LANES

    def page_copies(seq, slot):
        copies = []
        for p in range(n_pages):
            pg = pt_ref[seq, p]
            rows = pl.ds(p * page * n_heads, page * n_heads)
            copies.append(pltpu.make_async_copy(ck_hbm.at[layer, pg], kbuf.at[slot, rows], sem.at[0, slot]))
            copies.append(pltpu.make_async_copy(cv_hbm.at[layer, pg], vbuf.at[slot, rows], sem.at[1, slot]))
        return copies

    @pl.when(n == 0)
    def _():
        for c in page_copies(0, 0):
            c.start()

    @pl.when(n + 1 < pl.num_programs(0))
    def _():
        for c in page_copies(n + 1, (n + 1) % 2):
            c.start()

    slot = n % 2
    for c in page_copies(n, slot):
        c.wait()

    qrep = jnp.concatenate([q_ref[...]] * (LANES // dec_seq), axis=0)
    rowi = lax.broadcasted_iota(jnp.int32, (LANES, d_b), 0)
    lanei = lax.broadcasted_iota(jnp.int32, (LANES, d_b), 1)
    qbd = jnp.where(lanei // HEAD_W == rowi // dec_seq, qrep, 0.0).astype(BF16)

    pad = jnp.zeros((LANES - dec_seq, HEAD_W), F32)
    for h in range(n_heads):
        hs = slice(h * HEAD_W, (h + 1) * HEAD_W)
        head_rows = pl.ds(h, past, stride=n_heads)
        kb_sc[:past, hs] = kbuf.at[slot][head_rows, :].astype(BF16)
        vb_sc[:past, hs] = vbuf.at[slot][head_rows, :].astype(BF16)
        kb_sc[past:, hs] = jnp.concatenate([kn_ref[:, h, :], pad], axis=0).astype(BF16)
        vb_sc[past:, hs] = jnp.concatenate([vn_ref[:, h, :], pad], axis=0).astype(BF16)

    sc = lax.dot_general(kb_sc[...], qbd, (((1,), (1,)), ((), ())), preferred_element_type=F32) * ATTN_SCALE

    def expand(b):
        col = lax.broadcasted_iota(jnp.int32, (b.shape[0], LANES), 1)
        out = jnp.zeros((b.shape[0], LANES), F32)
        for h in range(n_heads):
            out = jnp.where(col // dec_seq == h, b[:, h:h + 1], out)
        return out

    bp = bp_ref[...]
    pieces = [jnp.broadcast_to(bp[h:h + 1, :], (dec_seq, past)) for h in range(n_heads)]
    if n_heads * dec_seq < LANES:
        pieces.append(jnp.zeros((LANES - n_heads * dec_seq, past), F32))
    b_rep = jnp.concatenate(pieces, axis=0)
    for c in range(past // LANES):
        cs = slice(c * LANES, (c + 1) * LANES)
        s_sc[cs, :] = sc[cs, :] + b_rep[:, cs].T
    krow = lax.broadcasted_iota(jnp.int32, (LANES, LANES), 0)
    qcol = lax.broadcasted_iota(jnp.int32, (LANES, LANES), 1) % dec_seq
    bn = jnp.concatenate([expand(bn_ref[...]), jnp.zeros((LANES - dec_seq, LANES), F32)], axis=0)
    s_sc[past:, :] = jnp.where(krow <= qcol, sc[past:, :] + bn, -jnp.inf)

    m = jnp.max(s_sc[...], axis=0, keepdims=True)
    l = jnp.zeros((LANES, 1), F32)
    for c in range(n_keys // LANES):
        cs = slice(c * LANES, (c + 1) * LANES)
        pt = jnp.exp(s_sc[cs, :] - m).T
        l = l + jnp.sum(pt, -1, keepdims=True)
        p_sc[:, cs] = pt.astype(BF16)
    acc = jnp.dot(p_sc[...], vb_sc[...], preferred_element_type=F32)
    for h in range(n_heads):
        rs = slice(h * dec_seq, (h + 1) * dec_seq)
        hs = slice(h * HEAD_W, (h + 1) * HEAD_W)
        o = acc[rs, hs] / l[rs, :]
        o_ref[:, hs] = _head_rms(o) * gm_ref[:, hs]


def _sample_attention(page_table, qs, k_new, v_new, bias_past, bias_new, gm_b, cache_k, cache_v, *,
                      layer, dec_seq):
    n_seq, n_pages = page_table.shape
    depth, n_pool, page, n_heads, head_w = cache_k.shape
    d_b = n_heads * head_w
    past = n_pages * page
    cache_k = cache_k.reshape(depth, n_pool, page * n_heads, head_w)
    cache_v = cache_v.reshape(depth, n_pool, page * n_heads, head_w)

    new_rows = pl.BlockSpec((None, dec_seq, n_heads, head_w), lambda n, pt: (layer, n, 0, 0))
    grid_spec = pltpu.PrefetchScalarGridSpec(
        num_scalar_prefetch=1,
        grid=(n_seq,),
        in_specs=[
            pl.BlockSpec((dec_seq, d_b), lambda n, pt: (n, 0)),
            new_rows,
            new_rows,
            pl.BlockSpec((None, n_heads, past), lambda n, pt: (n, 0, 0)),
            pl.BlockSpec((None, dec_seq, n_heads), lambda n, pt: (n, 0, 0)),
            pl.BlockSpec((1, d_b), lambda n, pt: (0, 0)),
            pl.BlockSpec(memory_space=pl.ANY),
            pl.BlockSpec(memory_space=pl.ANY),
        ],
        out_specs=pl.BlockSpec((dec_seq, d_b), lambda n, pt: (n, 0)),
        scratch_shapes=[
            pltpu.VMEM((2, past * n_heads, head_w), F32),
            pltpu.VMEM((2, past * n_heads, head_w), F32),
            pltpu.VMEM((past + LANES, d_b), BF16),
            pltpu.VMEM((past + LANES, d_b), BF16),
            pltpu.VMEM((past + LANES, LANES), F32),
            pltpu.VMEM((LANES, past + LANES), BF16),
            pltpu.SemaphoreType.DMA((2, 2)),
        ],
    )
    kern = functools.partial(_sample_attn_kernel, layer=layer, n_pages=n_pages, page=page, dec_seq=dec_seq)
    return pl.pallas_call(
        kern,
        grid_spec=grid_spec,
        out_shape=jax.ShapeDtypeStruct((n_seq * dec_seq, d_b), F32),
        compiler_params=_params(1, 56),
        name="sample_attn",
    )(page_table, qs, k_new, v_new, bias_past, bias_new, gm_b, cache_k, cache_v)


def _mix_kernel(ma_ref, mbp_ref, mbs_ref, mc_ref, x_ref, w_ref, g1_ref, b1_ref, wrh_ref, wrl_ref, br_ref,
                x1g_ref, gid_ref, mixed_sc, *, n_prompt_tiles, alpha):
    i = pl.program_id(0)
    tm, d = x_ref.shape
    d_a = ma_ref.shape[-1]
    d_b = mbp_ref.shape[-1]
    mixed_sc[:, :d_a] = ma_ref[...]
    mixed_sc[:, d_a + d_b:] = mc_ref[...]

    @pl.when(i < n_prompt_tiles)
    def _():
        mixed_sc[:, d_a:d_a + d_b] = mbp_ref[...]

    @pl.when(i >= n_prompt_tiles)
    def _():
        mixed_sc[:, d_a:d_a + d_b] = mbs_ref[...].astype(BF16)

    y = jnp.dot(mixed_sc[...], w_ref[...], preferred_element_type=F32)
    x1 = _layer_norm(alpha * x_ref[...] + y, g1_ref[...], b1_ref[...])
    x1g_ref[:, :d] = x1

    hi = x1.astype(BF16)
    lo = (x1 - hi.astype(F32)).astype(BF16)
    logits = (jnp.dot(hi, wrh_ref[...], preferred_element_type=F32)
              + jnp.dot(lo, wrh_ref[...], preferred_element_type=F32)
              + jnp.dot(hi, wrl_ref[...], preferred_element_type=F32))
    score_all = jax.nn.sigmoid(logits.T)
    sel_all = score_all + br_ref[...]
    score = [score_all[e:e + 1, :] for e in range(N_EXPERTS)]
    sel = [sel_all[e:e + 1, :] for e in range(N_EXPERTS)]

    gscore = []
    for g in range(N_GROUPS):
        a0, a1, a2, a3 = sel[4 * g:4 * g + 4]
        h1, l1 = jnp.maximum(a0, a1), jnp.minimum(a0, a1)
        h2, l2 = jnp.maximum(a2, a3), jnp.minimum(a2, a3)
        gscore.append(jnp.maximum(h1, h2) + jnp.maximum(jnp.minimum(h1, h2), jnp.maximum(l1, l2)))
    gmax = functools.reduce(jnp.maximum, gscore)
    gid = jnp.full(gscore[0].shape, N_GROUPS - 1, jnp.int32)
    for g in range(N_GROUPS - 2, -1, -1):
        gid = jnp.where(gscore[g] == gmax, g, gid)
    gid_ref[...] = gid

    chosen = []
    for e in range(N_EXPERTS):
        g = e // EXPERTS_PER_GROUP
        rank = jnp.zeros(sel[e].shape, jnp.int32)
        for o in range(4 * g, 4 * g + 4):
            if o < e:
                rank = rank + (sel[o] >= sel[e]).astype(jnp.int32)
            elif o > e:
                rank = rank + (sel[o] > sel[e]).astype(jnp.int32)
        chosen.append(jnp.logical_and(gid == g, rank < 2))
    wsum = functools.reduce(jnp.add, [jnp.where(chosen[e], score[e], 0.0) for e in range(N_EXPERTS)])
    rowi = lax.broadcasted_iota(jnp.int32, (N_EXPERTS, tm), 0)
    gates_t = jnp.zeros((N_EXPERTS, tm), F32)
    for e in range(N_EXPERTS):
        gates_t = jnp.where(rowi == e, jnp.where(chosen[e], score[e] / wsum, 0.0), gates_t)
    gates_t = jnp.concatenate([gates_t, jnp.zeros((LANES - N_EXPERTS, tm), F32)], axis=0)
    x1g_ref[:, d:] = gates_t.T


def _mix(ma, mbp, mbs, mc, x, w_out, g1, b1, wr_hi, wr_lo, br_col, *, tm, n_prompt_rows, alpha):
    r, d = x.shape
    d_a, d_b, d_c = ma.shape[1], mbp.shape[1], mc.shape[1]
    npt = n_prompt_rows // tm
    rows = lambda i: (i, 0)
    return pl.pallas_call(
        functools.partial(_mix_kernel, n_prompt_tiles=npt, alpha=alpha),
        grid=(r // tm,),
        in_specs=[
            pl.BlockSpec((tm, d_a), rows),
            pl.BlockSpec((tm, d_b), lambda i: (jnp.minimum(i, npt - 1), 0)),
            pl.BlockSpec((tm, d_b), lambda i: (jnp.maximum(i - npt, 0), 0)),
            pl.BlockSpec((tm, d_c), rows),
            pl.BlockSpec((tm, d), rows),
            _resident((d_a + d_b + d_c, d)),
            _resident((1, d)),
            _resident((1, d)),
            _resident((d, LANES)),
            _resident((d, LANES)),
            _resident((LANES, 1)),
        ],
        out_specs=[
            pl.BlockSpec((tm, d + LANES), rows),
            pl.BlockSpec((1, tm), lambda i: (0, i)),
        ],
        out_shape=[
            jax.ShapeDtypeStruct((r, d + LANES), F32),
            jax.ShapeDtypeStruct((1, r), jnp.int32),
        ],
        scratch_shapes=[pltpu.VMEM((tm, d_a + d_b + d_c), BF16)],
        compiler_params=_params(1, 40),
        name="mix",
    )(ma, mbp, mbs, mc, x, w_out, g1, b1, wr_hi, wr_lo, br_col)


def _moe_kernel(tok_ref, tg_ref, tr_ref, x1g_hbm, wg_ref, wu_ref, wd_ref, g2_ref, b2_ref, out_hbm,
                xg_sc, xb_sc, acc_sc, y_sc, sem, *, tm, d, n_rows, alpha):
    t = pl.program_id(0)
    j = pl.program_id(1)
    n_real = tr_ref[t]
    valid = n_real > 0
    slot = t % 2
    quarter = tm // EXPERTS_PER_GROUP

    def row_in(tile, r, sl):
        return pltpu.make_async_copy(x1g_hbm.at[pl.ds(tok_ref[tile * tm + r], 1), :],
                                     xg_sc.at[sl, pl.ds(r, 1), :], sem.at[sl])

    def wait_rows(sl):
        pltpu.make_async_copy(x1g_hbm.at[pl.ds(0, tm), :], xg_sc.at[sl], sem.at[sl]).wait()

    def row_out(tile, n_tile, r):
        dst = jnp.where(r < n_tile, tok_ref[tile * tm + r], n_rows + r)
        return pltpu.make_async_copy(y_sc.at[pl.ds(r, 1), :], out_hbm.at[pl.ds(dst, 1), :], sem.at[2])

    def wait_out():
        pltpu.make_async_copy(y_sc, out_hbm.at[pl.ds(0, tm), :], sem.at[2]).wait()

    def for_rows(n, fn, unroll):
        def body(r, carry):
            fn(r)
            return carry
        lax.fori_loop(0, n, body, 0, unroll=unroll)

    @pl.when(jnp.logical_and(j == 0, t == 0))
    def _():
        y_sc[...] = jnp.zeros_like(y_sc)
        spare = pltpu.make_async_copy(y_sc, out_hbm.at[pl.ds(n_rows, tm), :], sem.at[3])
        spare.start()
        spare.wait()
        for_rows(tm, lambda r: row_in(0, r, 0).start(), ROW_DMA_UNROLL)

    @pl.when(jnp.logical_and(valid, j == 0))
    def _():
        wait_rows(slot)
        xb_sc[...] = xg_sc[slot, :, :d].astype(BF16)
        acc_sc[...] = jnp.zeros_like(acc_sc)

    @pl.when(valid)
    def _():
        prev = jnp.maximum(t - 1, 0)
        n_prev = jnp.where(t > 0, tr_ref[prev], 0)
        for k in range(quarter):
            row_in(t + 1, j * quarter + k, 1 - slot).start()
            row_out(prev, n_prev, j * quarter + k).start()
        xb = xb_sc[...]
        hid = (jax.nn.silu(jnp.dot(xb, wg_ref[...].astype(BF16), preferred_element_type=F32))
               * jnp.dot(xb, wu_ref[...].astype(BF16), preferred_element_type=F32))
        lane = lax.broadcasted_iota(jnp.int32, (tm, LANES), 1)
        expert = tg_ref[t] * EXPERTS_PER_GROUP + j
        gate = jnp.sum(jnp.where(lane == expert, xg_sc[slot, :, d:], 0.0), -1, keepdims=True)
        acc_sc[...] += jnp.dot((hid * gate).astype(BF16), wd_ref[...].astype(BF16), preferred_element_type=F32)

    @pl.when(jnp.logical_and(valid, j == EXPERTS_PER_GROUP - 1))
    def _():
        wait_out()
        y_sc[...] = _layer_norm(alpha * xg_sc[slot, :, :d] + acc_sc[...], g2_ref[...], b2_ref[...])

        @pl.when(tr_ref[t + 1] == 0)
        def _():
            for_rows(tm, lambda r: row_out(t, n_real, r).start(), ROW_DMA_UNROLL)
            wait_out()
            wait_rows(1 - slot)


def _moe(token, tile_group, tile_rows, x1g, wg, wu, wd, g2, b2, *, layer, tm, n_rows, alpha):
    d = x1g.shape[1] - LANES
    d_ff = wg.shape[-1]
    n_tiles = tile_group.shape[0]

    def expert(t, j, tok, tg, tr):
        return layer, tg[t] * EXPERTS_PER_GROUP + jnp.where(tr[t] > 0, j, EXPERTS_PER_GROUP - 1), 0, 0

    grid_spec = pltpu.PrefetchScalarGridSpec(
        num_scalar_prefetch=3,
        grid=(n_tiles, EXPERTS_PER_GROUP),
        in_specs=[
            pl.BlockSpec(memory_space=pl.ANY),
            pl.BlockSpec((None, None, d, d_ff), expert),
            pl.BlockSpec((None, None, d, d_ff), expert),
            pl.BlockSpec((None, None, d_ff, d), expert),
            pl.BlockSpec((1, d), lambda *a: (0, 0)),
            pl.BlockSpec((1, d), lambda *a: (0, 0)),
        ],
        out_specs=pl.BlockSpec(memory_space=pl.ANY),
        scratch_shapes=[
            pltpu.VMEM((2, tm, d + LANES), F32),
            pltpu.VMEM((tm, d), BF16),
            pltpu.VMEM((tm, d), F32),
            pltpu.VMEM((tm, d), F32),
            pltpu.SemaphoreType.DMA((4,)),
        ],
    )
    return pl.pallas_call(
        functools.partial(_moe_kernel, tm=tm, d=d, n_rows=n_rows, alpha=alpha),
        grid_spec=grid_spec,
        out_shape=jax.ShapeDtypeStruct((n_rows + tm, d), F32),
        compiler_params=_params(2, 56),
        name="moe",
    )(token, tile_group, tile_rows, x1g, wg, wu, wd, g2, b2)


def _sort_by_group(gid, *, tm, n_tiles):
    r = gid.shape[0]
    onehot = (gid[:, None] == jnp.arange(N_GROUPS, dtype=jnp.int32)[None, :]).astype(jnp.int32)
    rank = jnp.sum((jnp.cumsum(onehot, axis=0) - onehot) * onehot, axis=1)
    counts = jnp.sum(onehot, axis=0)
    tiles = (counts + tm - 1) // tm
    tile_end = jnp.cumsum(tiles)
    tile_start = tile_end - tiles
    slot = tile_start[gid] * tm + rank
    token = jnp.zeros(((n_tiles + 1) * tm,), jnp.int32).at[slot].set(jnp.arange(r, dtype=jnp.int32))
    tile_ids = jnp.arange(n_tiles, dtype=jnp.int32)
    tile_group = jnp.minimum(jnp.sum(tile_ids[:, None] >= tile_end[None, :], axis=1), N_GROUPS - 1).astype(jnp.int32)
    tile_rows = jnp.clip(counts[tile_group] - (tile_ids - tile_start[tile_group]) * tm, 0, tm)
    tile_rows = jnp.where(tile_ids < tile_end[-1], tile_rows, 0).astype(jnp.int32)
    last_group = tile_group[jnp.maximum(tile_end[-1] - 1, 0)]
    tile_group = jnp.where(tile_rows > 0, tile_group, last_group)
    return token, tile_group, jnp.pad(tile_rows, (0, 1))


def _embed_kernel(x_ref, p_ref, wg_ref, bg_ref, wp_ref, *o_refs, n_prompt_tiles, split):
    x = x_ref[...]
    gate = jax.nn.sigmoid(jnp.dot(x.astype(BF16), wg_ref[...], preferred_element_type=F32) + bg_ref[...])
    proj = jnp.dot(p_ref[...].astype(BF16), wp_ref[...], preferred_element_type=F32)
    y = x + gate * proj
    if split:
        @pl.when(pl.program_id(0) < n_prompt_tiles)
        def _():
            o_refs[0][...] = y

        @pl.when(pl.program_id(0) >= n_prompt_tiles)
        def _():
            o_refs[1][...] = y
    else:
        o_refs[0][...] = y
        o_refs[1][...] = y.astype(BF16)


def _embed(x2, p, w_pg, b_pg, w_pp, *, tm, n_rows, n_prompt_rows, split):
    d = x2.shape[1]
    d_p = p.shape[1]
    npt = n_prompt_rows // tm
    rows = lambda i: (i, 0)
    if split:
        out_specs = [pl.BlockSpec((tm, d), lambda i: (jnp.minimum(i, npt - 1), 0)),
                     pl.BlockSpec((tm, d), lambda i: (jnp.maximum(i - npt, 0), 0))]
        out_shape = [jax.ShapeDtypeStruct((n_prompt_rows, d), F32),
                     jax.ShapeDtypeStruct((n_rows - n_prompt_rows, d), F32)]
    else:
        out_specs = [pl.BlockSpec((tm, d), rows), pl.BlockSpec((tm, d), rows)]
        out_shape = [jax.ShapeDtypeStruct((n_rows, d), F32), jax.ShapeDtypeStruct((n_rows, d), BF16)]
    return pl.pallas_call(
        functools.partial(_embed_kernel, n_prompt_tiles=npt, split=split),
        grid=(n_rows // tm,),
        in_specs=[
            pl.BlockSpec((tm, d), rows),
            pl.BlockSpec((tm, d_p), rows),
            _resident((d, d)),
            _resident((1, d)),
            _resident((d_p, d)),
        ],
        out_specs=out_specs,
        out_shape=out_shape,
        compiler_params=_params(1, 40),
        name="embed",
    )(x2, p, w_pg, b_pg, w_pp)


def kernel(x_prompt, x_sample, p_prompt, p_sample, cache_k, cache_v, cache_logf, state_conv, page_table, w_in, b_f, g_v, w_s, b_s, conv_w, g_mix, w_out, ln1_g, ln1_b, w_router, b_router, w_gate, w_up, w_down, ln2_g, ln2_b, w_ple_gate, b_ple_gate, w_ple_proj):
    depth = w_in.shape[0]
    batch, seq, d = x_prompt.shape
    n_seq, dec_seq, _ = x_sample.shape
    n_heads = cache_k.shape[3]
    d_a = g_v.shape[-1]
    d_b = n_heads * HEAD_W
    d_c = conv_w.shape[-1]
    np_rows = batch * seq
    ns_rows = n_seq * dec_seq
    n_rows = np_rows + ns_rows
    alpha = (2 * depth) ** 0.25
    tm = 256
    tm_moe = 512
    tq = 512
    assert batch == 1 and CONV_W - 1 <= dec_seq <= CHUNK and dec_seq & (dec_seq - 1) == 0
    assert np_rows % tq == 0 and ns_rows % tm == 0 and np_rows % tm_moe == 0 and ns_rows % tm_moe == 0
    assert w_s.shape[-1] == CHUNK and cache_k.shape[-1] == HEAD_W and n_heads <= SUBLANES
    n_sub = 4
    assert (n_seq * n_heads) % tm == 0 and (np_rows // tq) % n_sub == 0
    assert w_gate.shape[1] == N_EXPERTS and w_in.shape[-1] == 2 * d_a + 3 * d_b + n_heads + 3 * d_c
    n_tiles = n_rows // tm_moe + N_GROUPS

    x = jnp.concatenate([x_prompt.reshape(np_rows, d), x_sample.reshape(ns_rows, d)], axis=0)
    xb = x.astype(BF16)
    wr_pad = jnp.pad(w_router, ((0, 0), (0, LANES - N_EXPERTS)))
    wr_hi = wr_pad.astype(BF16)
    wr_lo = (wr_pad - wr_hi.astype(F32)).astype(BF16)
    br_col = jnp.pad(b_router, (0, LANES - N_EXPERTS)).reshape(LANES, 1)
    reps = CHUNK // dec_seq

    k_prompt = jnp.zeros((depth, np_rows, n_heads, HEAD_W), F32)
    v_prompt = jnp.zeros((depth, np_rows, n_heads, HEAD_W), F32)
    k_sample = jnp.zeros((depth, ns_rows, n_heads, HEAD_W), F32)
    v_sample = jnp.zeros((depth, ns_rows, n_heads, HEAD_W), F32)
    outs = [[] for _ in range(5)]
    for i in range(depth):
        o_q, o_k, o_v = 2 * d_a, 2 * d_a + d_b, 2 * d_a + 2 * d_b
        o_f = 2 * d_a + 3 * d_b
        o_c = o_f + n_heads
        wi = w_in[i]
        w_a = wi[:, :o_q].astype(BF16)
        w_q = wi[:, o_q:o_k].astype(BF16)
        w_k = wi[:, o_k:o_v].astype(BF16)
        w_v = wi[:, o_v:o_f].astype(BF16)
        w_c = jnp.concatenate([wi[:, o_c:], jnp.pad(wi[:, o_f:o_c], ((0, 0), (0, LANES - n_heads)))], axis=1).astype(BF16)
        bf_pad = jnp.pad(b_f[i], (0, LANES - n_heads)).reshape(1, LANES)
        gm = g_mix[i].reshape(1, -1)
        gm_a, gm_b, gm_c = gm[:, :d_a], gm[:, d_a:d_a + d_b], gm[:, d_a + d_b:]
        ws_all = jnp.stack([w_s[i], jnp.tile(w_s[i][:, :dec_seq, :dec_seq], (1, reps, reps))])
        bs_rows = lambda b: jnp.repeat(b.T, HEAD_W, axis=1)
        bs_all = jnp.stack([bs_rows(b_s[i]), bs_rows(jnp.tile(b_s[i][:, :dec_seq], (1, reps)))])
        st = state_conv[i]
        zero_rows = jnp.zeros((n_seq, dec_seq - 1, d_c), F32)
        s1 = jnp.concatenate([st[:, 1:2], zero_rows], axis=1).reshape(ns_rows, d_c)
        s2 = jnp.concatenate([st[:, 0:1], st[:, 1:2], zero_rows[:, 1:]], axis=1).reshape(ns_rows, d_c)

        av_s, ma = _branch_a(xb, w_a, g_v[i].reshape(1, d_a), ws_all, bs_all, gm_a,
                             tm=tm, n_prompt_rows=np_rows, dec_seq=dec_seq)
        qt, qs = _project(xb, w_q, tm=tq, n_prompt_rows=np_rows, t_scale=ATTN_SCALE * LOG2E, sample_rows=True)
        kb, k_prompt, k_sample = _project(xb, w_k, tm=tq, n_prompt_rows=np_rows, flat=True,
                                          cache=(k_prompt, k_sample, i))
        vt, v_prompt, v_sample = _project(xb, w_v, tm=tq, n_prompt_rows=np_rows, t_scale=1.0,
                                          cache=(v_prompt, v_sample, i))
        zc, mc, lf, kbias = _branch_c(xb, w_c, bf_pad, conv_w[i], gm_c, s1, s2,
                                      tm=tm, n_prompt_rows=np_rows, dec_seq=dec_seq, d_b=d_b)
        mbp = _prompt_attention(qt, kb, kbias, vt, gm_b, n_prompt_rows=np_rows, tq=tq, n_sub=n_sub)

        lf_new = lf[np_rows:, :n_heads].reshape(n_seq, dec_seq, n_heads)
        pages_t = jnp.swapaxes(cache_logf[i], 1, 2)[page_table]
        lf_past_t = jnp.swapaxes(pages_t, 1, 2).reshape(n_seq, n_heads, -1)
        past = lf_past_t.shape[2]
        new_total = jnp.sum(lf_new, axis=1)
        bias_new = new_total[:, None, :] - jnp.cumsum(lf_new, axis=1)
        bias_past = _suffix_bias(lf_past_t.reshape(n_seq * n_heads, past), new_total.reshape(-1, 1), tm=tm)
        mbs = _sample_attention(page_table, qs, k_sample, v_sample, bias_past.reshape(n_seq, n_heads, past),
                                bias_new, gm_b, cache_k, cache_v, layer=i, dec_seq=dec_seq)

        x1g, gid = _mix(ma, mbp, mbs, mc, x, w_out[i].astype(BF16), ln1_g[i].reshape(1, d), ln1_b[i].reshape(1, d),
                        wr_hi, wr_lo, br_col, tm=tm, n_prompt_rows=np_rows, alpha=alpha)
        token, tile_group, tile_rows = _sort_by_group(gid[0], tm=tm_moe, n_tiles=n_tiles)
        x2 = _moe(token, tile_group, tile_rows, x1g, w_gate, w_up, w_down,
                  ln2_g[i].reshape(1, d), ln2_b[i].reshape(1, d),
                  layer=i, tm=tm_moe, n_rows=n_rows, alpha=alpha)
        p = jnp.concatenate([p_prompt[i].reshape(np_rows, -1), p_sample[i].reshape(ns_rows, -1)], axis=0)
        last = i == depth - 1
        y = _embed(x2, p, w_ple_gate[i].astype(BF16), b_ple_gate[i].reshape(1, d), w_ple_proj[i].astype(BF16),
                   tm=tm, n_rows=n_rows, n_prompt_rows=np_rows, split=last)
        if not last:
            x, xb = y

        outs[0].append(lf[:np_rows, :n_heads].reshape(batch, seq, n_heads))
        outs[1].append(zc[np_rows - (CONV_W - 1):np_rows].reshape(batch, CONV_W - 1, d_c))
        outs[2].append(av_s.reshape(n_seq, dec_seq, d_a))
        outs[3].append(lf_new)
        outs[4].append(zc[np_rows:].reshape(n_seq, dec_seq, d_c)[:, dec_seq - (CONV_W - 1):])

    new_lf_p, new_conv_p, new_av_s, new_lf_s, new_conv_s = (jnp.stack(o) for o in outs)
    return (y[0].reshape(batch, seq, d), y[1].reshape(n_seq, dec_seq, d),
            k_prompt.reshape(depth, batch, seq, n_heads, HEAD_W), v_prompt.reshape(depth, batch, seq, n_heads, HEAD_W),
            new_lf_p, new_conv_p, new_av_s,
            k_sample.reshape(depth, n_seq, dec_seq, n_heads, HEAD_W),
            v_sample.reshape(depth, n_seq, dec_seq, n_heads, HEAD_W),
            new_lf_s, new_conv_s)
```

```python
import functools

import numpy as np
import jax
import jax.numpy as jnp
from jax import lax
from jax.experimental import pallas as pl
from jax.experimental.pallas import tpu as pltpu

F32 = jnp.float32
BF16 = jnp.bfloat16

HEAD_W = 128
CHUNK = 128
CONV_W = 3
N_GROUPS = 4
EXPERTS_PER_GROUP = 4
N_EXPERTS = N_GROUPS * EXPERTS_PER_GROUP
LN_EPS = 1e-5
RMS_EPS = 1e-6
ATTN_SCALE = HEAD_W ** -0.5
LOG2E = float(np.log2(np.e))
LANES = 128
SUBLANES = 8
MIB = 1024 * 1024
ROW_DMA_UNROLL = 8


def _params(n_axes, vmem_mib):
    return pltpu.CompilerParams(dimension_semantics=("arbitrary",) * n_axes,
                                vmem_limit_bytes=vmem_mib * MIB)


def _resident(shape):
    zeros = (0,) * len(shape)
    return pl.BlockSpec(shape, lambda *_: zeros, pipeline_mode=pl.Buffered(1))


def _layer_norm(v, gain, bias=None):
    mu = jnp.mean(v, -1, keepdims=True)
    var = jnp.mean(jnp.square(v - mu), -1, keepdims=True)
    y = (v - mu) * lax.rsqrt(var + LN_EPS) * gain
    return y if bias is None else y + bias


def _head_rms(o):
    return o * lax.rsqrt(jnp.mean(o * o, -1, keepdims=True) + RMS_EPS)


def _split3(v):
    hi = v.astype(BF16)
    r1 = v - hi.astype(F32)
    mid = r1.astype(BF16)
    lo = (r1 - mid.astype(F32)).astype(BF16)
    return hi, mid, lo


def _proj_kernel(*refs, n_prompt_tiles, flat, t_scale, sample_rows, cache):
    refs = iter(refs)
    x_ref, w_ref = next(refs), next(refs)
    if cache:
        next(refs), next(refs)
    i = pl.program_id(0)
    z = jnp.dot(x_ref[...], w_ref[...], preferred_element_type=F32)
    if flat:
        next(refs)[...] = z.astype(BF16)
    if t_scale is not None:
        next(refs)[...] = (z * t_scale).T.astype(BF16)
    if sample_rows:
        next(refs)[...] = z
    if cache:
        prompt_ref, sample_ref = next(refs), next(refs)

        heads = z.shape[1] // HEAD_W

        def heads_to(o_ref):
            for h in range(heads):
                o_ref[pl.ds(h, z.shape[0], stride=heads), :] = z[:, h * HEAD_W:(h + 1) * HEAD_W]

        @pl.when(i < n_prompt_tiles)
        def _():
            heads_to(prompt_ref)

        @pl.when(i >= n_prompt_tiles)
        def _():
            heads_to(sample_ref)


def _project(x, w, *, tm, n_prompt_rows, flat=False, t_scale=None, sample_rows=False, cache=None):
    r, d = x.shape
    n = w.shape[1]
    npt = n_prompt_rows // tm
    srows = lambda i: (jnp.maximum(i - npt, 0), 0)
    operands = [x, w]
    in_specs = [pl.BlockSpec((tm, d), lambda i: (i, 0)), _resident((d, n))]
    out_shape, out_specs, aliases = [], [], {}
    if flat:
        out_shape.append(jax.ShapeDtypeStruct((r, n), BF16))
        out_specs.append(pl.BlockSpec((tm, n), lambda i: (i, 0)))
    if t_scale is not None:
        out_shape.append(jax.ShapeDtypeStruct((r // tm, n, tm), BF16))
        out_specs.append(pl.BlockSpec((None, n, tm), lambda i: (i, 0, 0)))
    if sample_rows:
        out_shape.append(jax.ShapeDtypeStruct((r - n_prompt_rows, n), F32))
        out_specs.append(pl.BlockSpec((tm, n), srows))
    if cache is not None:
        prompt_buf, sample_buf, layer = cache
        heads = n // HEAD_W
        for buf, tile in ((prompt_buf, lambda i: jnp.minimum(i, npt - 1)), (sample_buf, lambda i: jnp.maximum(i - npt, 0))):
            aliases[len(operands)] = len(out_shape)
            operands.append(buf)
            in_specs.append(pl.BlockSpec(memory_space=pl.ANY))
            out_shape.append(jax.ShapeDtypeStruct(buf.shape, buf.dtype))
            out_specs.append(pl.BlockSpec((None, tm * heads, HEAD_W), functools.partial(
                lambda i, tile: (layer, tile(i), 0), tile=tile)))
    kern = functools.partial(_proj_kernel, n_prompt_tiles=npt, flat=flat, t_scale=t_scale,
                             sample_rows=sample_rows, cache=cache is not None)
    return pl.pallas_call(
        kern,
        grid=(r // tm,),
        in_specs=in_specs,
        out_specs=out_specs,
        out_shape=out_shape,
        input_output_aliases=aliases,
        compiler_params=_params(1, 40),
        name="proj",
    )(*operands)


def _branch_a_kernel(x_ref, w_ref, gv_ref, ws_ref, bs_ref, gm_ref, av_ref, ma_ref, *,
                     n_prompt_tiles, tm, dec_shift):
    is_sample = pl.program_id(0) >= n_prompt_tiles
    d_a = gv_ref.shape[-1]
    z = jnp.dot(x_ref[...], w_ref[...], preferred_element_type=F32)
    u = jax.nn.gelu(z[:, :d_a])
    av = _layer_norm(jax.nn.gelu(z[:, d_a:]), gv_ref[...])
    av_ref[...] = av
    avb = av.astype(BF16)
    row = lax.broadcasted_iota(jnp.int32, (CHUNK, CHUNK), 0)
    col = lax.broadcasted_iota(jnp.int32, (CHUNK, CHUNK), 1)
    shift = jnp.where(is_sample, dec_shift, int(np.log2(CHUNK)))
    mask = jnp.logical_and(col <= row, jnp.right_shift(row, shift) == jnp.right_shift(col, shift))
    for h in range(d_a // HEAD_W):
        hs = slice(h * HEAD_W, (h + 1) * HEAD_W)
        wm = jnp.where(mask, ws_ref[h], 0.0).astype(BF16)
        for c in range(tm // CHUNK):
            cs = slice(c * CHUNK, (c + 1) * CHUNK)
            mix = jnp.dot(wm, avb[cs, hs], preferred_element_type=F32) + bs_ref[:, hs]
            o = _head_rms(u[cs, hs] * mix) * gm_ref[:, hs]
            ma_ref[cs, hs] = o.astype(BF16)


def _branch_a(x, w_a, g_v, ws_all, bs_all, gm_a, *, tm, n_prompt_rows, dec_seq):
    r, d = x.shape
    d_a = g_v.shape[-1]
    npt = n_prompt_rows // tm
    n_heads = d_a // HEAD_W
    kern = functools.partial(_branch_a_kernel, n_prompt_tiles=npt, tm=tm, dec_shift=int(np.log2(dec_seq)))
    sel = lambda i: (i >= npt).astype(jnp.int32)
    return pl.pallas_call(
        kern,
        grid=(r // tm,),
        in_specs=[
            pl.BlockSpec((tm, d), lambda i: (i, 0)),
            _resident((d, 2 * d_a)),
            _resident((1, d_a)),
            pl.BlockSpec((None, n_heads, CHUNK, CHUNK), lambda i: (sel(i), 0, 0, 0)),
            pl.BlockSpec((None, CHUNK, d_a), lambda i: (sel(i), 0, 0)),
            _resident((1, d_a)),
        ],
        out_specs=[
            pl.BlockSpec((tm, d_a), lambda i: (jnp.maximum(i - npt, 0), 0)),
            pl.BlockSpec((tm, d_a), lambda i: (i, 0)),
        ],
        out_shape=[
            jax.ShapeDtypeStruct((r - n_prompt_rows, d_a), F32),
            jax.ShapeDtypeStruct((r, d_a), BF16),
        ],
        compiler_params=_params(1, 32),
        name="branch_a",
    )(x, w_a, g_v, ws_all, bs_all, gm_a)


def _branch_c_kernel(x_ref, w_ref, bf_ref, cw_ref, gm_ref, s1_ref, s2_ref,
                     zc_ref, mc_ref, lf_ref, kb_ref, carry_z, carry_c, *,
                     n_prompt_tiles, tm, dec_seq):
    i = pl.program_id(0)
    d_c = gm_ref.shape[-1]

    @pl.when(i == 0)
    def _():
        carry_z[...] = jnp.zeros_like(carry_z)
        carry_c[...] = jnp.zeros_like(carry_c)

    z = jnp.dot(x_ref[...], w_ref[...], preferred_element_type=F32)
    c_b = z[:, :d_c]
    zc = z[:, d_c:2 * d_c] * z[:, 2 * d_c:3 * d_c]
    zc_ref[...] = zc

    lf = jax.nn.log_sigmoid(z[:, 3 * d_c:] + bf_ref[...])
    lf_ref[...] = lf
    trow = lax.broadcasted_iota(jnp.int32, (tm, tm), 0)
    tcol = lax.broadcasted_iota(jnp.int32, (tm, tm), 1)
    tri = jnp.where(tcol <= trow, 1.0, 0.0).astype(BF16)
    hi, mid, lo = _split3(lf)
    cs = (jnp.dot(tri, hi, preferred_element_type=F32) + jnp.dot(tri, mid, preferred_element_type=F32)
          + jnp.dot(tri, lo, preferred_element_type=F32)) + carry_c[...]
    carry_c[...] = cs[tm - 1:tm, :]
    neg = cs * (-LOG2E)
    lane = lax.broadcasted_iota(jnp.int32, (tm, HEAD_W), 1)
    for h in range(kb_ref.shape[-1] // HEAD_W):
        hi, mid, lo = (t.astype(F32) for t in _split3(jnp.broadcast_to(neg[:, h:h + 1], (tm, HEAD_W))))
        terms = jnp.where(lane == 0, hi, jnp.where(lane == 1, mid, jnp.where(lane == 2, lo, 0.0)))
        kb_ref[:, h * HEAD_W:(h + 1) * HEAD_W] = terms.astype(BF16)

    row = lax.broadcasted_iota(jnp.int32, (tm, d_c), 0)
    back1 = pltpu.roll(zc, 1, 0)
    back2 = pltpu.roll(zc, 2, 0)

    def finish(zm1, zm2):
        y = cw_ref[0:1, :] * zm2 + cw_ref[1:2, :] * zm1 + cw_ref[2:3, :] * zc
        o = c_b * y
        for g in range(d_c // HEAD_W):
            gs = slice(g * HEAD_W, (g + 1) * HEAD_W)
            mc_ref[:, gs] = (_head_rms(o[:, gs]) * gm_ref[:, gs]).astype(BF16)

    @pl.when(i < n_prompt_tiles)
    def _():
        p6 = carry_z[SUBLANES - 2:SUBLANES - 1, :]
        p7 = carry_z[SUBLANES - 1:SUBLANES, :]
        finish(jnp.where(row >= 1, back1, p7),
               jnp.where(row >= 2, back2, jnp.where(row == 0, p6, p7)))

    @pl.when(i >= n_prompt_tiles)
    def _():
        t = jnp.bitwise_and(row, dec_seq - 1)
        finish(jnp.where(t >= 1, back1, s1_ref[...]), jnp.where(t >= 2, back2, s2_ref[...]))

    carry_z[...] = zc[tm - SUBLANES:tm, :]


def _branch_c(x, w_c, bf_pad, conv_w, gm_c, s1, s2, *, tm, n_prompt_rows, dec_seq, d_b):
    r, d = x.shape
    d_c = gm_c.shape[-1]
    n = w_c.shape[1]
    npt = n_prompt_rows // tm
    kern = functools.partial(_branch_c_kernel, n_prompt_tiles=npt, tm=tm, dec_seq=dec_seq)
    rows = lambda i: (i, 0)
    srows = lambda i: (jnp.maximum(i - npt, 0), 0)
    return pl.pallas_call(
        kern,
        grid=(r // tm,),
        in_specs=[
            pl.BlockSpec((tm, d), rows),
            _resident((d, n)),
            _resident((1, LANES)),
            _resident((CONV_W, d_c)),
            _resident((1, d_c)),
            pl.BlockSpec((tm, d_c), srows),
            pl.BlockSpec((tm, d_c), srows),
        ],
        out_specs=[
            pl.BlockSpec((tm, d_c), rows),
            pl.BlockSpec((tm, d_c), rows),
            pl.BlockSpec((tm, LANES), rows),
            pl.BlockSpec((tm, d_b), rows),
        ],
        out_shape=[
            jax.ShapeDtypeStruct((r, d_c), F32),
            jax.ShapeDtypeStruct((r, d_c), BF16),
            jax.ShapeDtypeStruct((r, LANES), F32),
            jax.ShapeDtypeStruct((r, d_b), BF16),
        ],
        scratch_shapes=[pltpu.VMEM((SUBLANES, d_c), F32), pltpu.VMEM((1, LANES), F32)],
        compiler_params=_params(1, 40),
        name="branch_c",
    )(x, w_c, bf_pad, conv_w, gm_c, s1, s2)


def _prompt_attn_kernel(qt_ref, k_ref, kb_ref, vt_ref, gm_ref, o_ref, m_sc, l_sc, acc_sc, *, tq):
    qi = pl.program_id(1)
    n_sub = qt_ref.shape[0]
    sub = lax.broadcasted_iota(jnp.int32, (HEAD_W, tq), 0)
    ones = jnp.where(sub < 3, 1.0, 0.0).astype(BF16)
    q_aug = [jnp.concatenate([qt_ref[s], ones], axis=0) for s in range(n_sub)]
    m_sc[...] = jnp.full_like(m_sc, -jnp.inf)
    l_sc[...] = jnp.zeros_like(l_sc)
    acc_sc[...] = jnp.zeros_like(acc_sc)

    def block(j, s, diagonal):
        k_aug = jnp.concatenate([k_ref[j], kb_ref[j]], axis=1)
        st = jnp.dot(k_aug, q_aug[s], preferred_element_type=F32)
        if diagonal:
            key = lax.broadcasted_iota(jnp.int32, (tq, tq), 0)
            query = lax.broadcasted_iota(jnp.int32, (tq, tq), 1)
            st = jnp.where(key <= query, st, -jnp.inf)
        m_old = m_sc[s]
        m_new = jnp.maximum(m_old, jnp.max(st, axis=0, keepdims=True))
        alpha = jnp.exp2(m_old - m_new)
        pt = jnp.exp2(st - m_new)
        l_sc[s] = alpha * l_sc[s] + jnp.sum(pt, axis=0, keepdims=True)
        acc_sc[s] = alpha * acc_sc[s] + jnp.dot(vt_ref[j], pt.astype(BF16), preferred_element_type=F32)
        m_sc[s] = m_new

    def body(j, carry):
        for s in range(n_sub):
            block(j, s, False)
        return carry

    lax.fori_loop(0, n_sub * qi, body, 0)
    for s in range(n_sub):
        for e in range(s + 1):
            block(n_sub * qi + e, s, e == s)
        o = (acc_sc[s] / l_sc[s]).T
        o_ref[s * tq:(s + 1) * tq, :] = (_head_rms(o) * gm_ref[...]).astype(o_ref.dtype)


def _prompt_attention(qt, kb, kbias, vt, gm_b, *, n_prompt_rows, tq, n_sub):
    d_b = kb.shape[1]
    n_heads = d_b // HEAD_W
    nk = n_prompt_rows // tq
    kb3 = kb.reshape(-1, tq, d_b)
    kbias3 = kbias.reshape(-1, tq, d_b)
    key_rows = pl.BlockSpec((nk, tq, HEAD_W), lambda h, qi: (0, 0, h))
    return pl.pallas_call(
        functools.partial(_prompt_attn_kernel, tq=tq),
        grid=(n_heads, nk // n_sub),
        in_specs=[
            pl.BlockSpec((n_sub, HEAD_W, tq), lambda h, qi: (qi, h, 0)),
            key_rows,
            key_rows,
            pl.BlockSpec((nk, HEAD_W, tq), lambda h, qi: (0, h, 0)),
            pl.BlockSpec((1, HEAD_W), lambda h, qi: (0, h)),
        ],
        out_specs=pl.BlockSpec((n_sub * tq, HEAD_W), lambda h, qi: (qi, h)),
        out_shape=jax.ShapeDtypeStruct((n_prompt_rows, d_b), BF16),
        scratch_shapes=[pltpu.VMEM((n_sub, 1, tq), F32), pltpu.VMEM((n_sub, 1, tq), F32),
                        pltpu.VMEM((n_sub, HEAD_W, tq), F32)],
        compiler_params=_params(2, 32),
        name="prompt_attn",
    )(qt, kb3, kbias3, vt, gm_b)


def _suffix_kernel(lf_ref, tail_ref, o_ref):
    past = lf_ref.shape[1]
    u = lax.broadcasted_iota(jnp.int32, (LANES, LANES), 0)
    s = lax.broadcasted_iota(jnp.int32, (LANES, LANES), 1)
    later = jnp.where(u > s, 1.0, 0.0).astype(BF16)
    carry = tail_ref[...]
    for c in reversed(range(past // LANES)):
        cs = slice(c * LANES, (c + 1) * LANES)
        x = lf_ref[:, cs]
        hi, mid, lo = _split3(x)
        within = (jnp.dot(hi, later, preferred_element_type=F32) + jnp.dot(mid, later, preferred_element_type=F32)
                  + jnp.dot(lo, later, preferred_element_type=F32))
        o_ref[:, cs] = within + carry
        carry = carry + jnp.sum(x, -1, keepdims=True)


def _suffix_bias(lf_rows, tail, *, tm):
    rows, past = lf_rows.shape
    return pl.pallas_call(
        _suffix_kernel,
        grid=(rows // tm,),
        in_specs=[pl.BlockSpec((tm, past), lambda i: (i, 0)), pl.BlockSpec((tm, 1), lambda i: (i, 0))],
        out_specs=pl.BlockSpec((tm, past), lambda i: (i, 0)),
        out_shape=jax.ShapeDtypeStruct((rows, past), F32),
        compiler_params=_params(1, 32),
        name="suffix_bias",
    )(lf_rows, tail)


def _sample_attn_kernel(pt_ref, q_ref, kn_ref, vn_ref, bp_ref, bn_ref, gm_ref, ck_hbm, cv_hbm, o_ref,
                        kbuf, vbuf, kb_sc, vb_sc, s_sc, p_sc, sem, *, layer, n_pages, page, dec_seq):
    n = pl.program_id(0)
    d_b = q_ref.shape[-1]
    n_heads = d_b // HEAD_W
    past = n_pages * page
    n_keys = past + LANES

    def page_copies(seq, slot):
        copies = []
        for p in range(n_pages):
            pg = pt_ref[seq, p]
            rows = pl.ds(p * page * n_heads, page * n_heads)
            copies.append(pltpu.make_async_copy(ck_hbm.at[layer, pg], kbuf.at[slot, rows], sem.at[0, slot]))
            copies.append(pltpu.make_async_copy(cv_hbm.at[layer, pg], vbuf.at[slot, rows], sem.at[1, slot]))
        return copies

    @pl.when(n == 0)
    def _():
        for c in page_copies(0, 0):
            c.start()

    @pl.when(n + 1 < pl.num_programs(0))
    def _():
        for c in page_copies(n + 1, (n + 1) % 2):
            c.start()

    slot = n % 2
    for c in page_copies(n, slot):
        c.wait()

    qrep = jnp.concatenate([q_ref[...]] * (LANES // dec_seq), axis=0)
    rowi = lax.broadcasted_iota(jnp.int32, (LANES, d_b), 0)
    lanei = lax.broadcasted_iota(jnp.int32, (LANES, d_b), 1)
    qbd = jnp.where(lanei // HEAD_W == rowi // dec_seq, qrep, 0.0).astype(BF16)

    pad = jnp.zeros((LANES - dec_seq, HEAD_W), F32)
    for h in range(n_heads):
        hs = slice(h * HEAD_W, (h + 1) * HEAD_W)
        head_rows = pl.ds(h, past, stride=n_heads)
        kb_sc[:past, hs] = kbuf.at[slot][head_rows, :].astype(BF16)
        vb_sc[:past, hs] = vbuf.at[slot][head_rows, :].astype(BF16)
        new_rows = pl.ds(h, dec_seq, stride=n_heads)
        kb_sc[past:, hs] = jnp.concatenate([kn_ref[new_rows, :], pad], axis=0).astype(BF16)
        vb_sc[past:, hs] = jnp.concatenate([vn_ref[new_rows, :], pad], axis=0).astype(BF16)

    sc = lax.dot_general(kb_sc[...], qbd, (((1,), (1,)), ((), ())), preferred_element_type=F32) * ATTN_SCALE

    def expand(b):
        col = lax.broadcasted_iota(jnp.int32, (b.shape[0], LANES), 1)
        out = jnp.zeros((b.shape[0], LANES), F32)
        for h in range(n_heads):
            out = jnp.where(col // dec_seq == h, b[:, h:h + 1], out)
        return out

    bp = bp_ref[...]
    pieces = [jnp.broadcast_to(bp[h:h + 1, :], (dec_seq, past)) for h in range(n_heads)]
    if n_heads * dec_seq < LANES:
        pieces.append(jnp.zeros((LANES - n_heads * dec_seq, past), F32))
    b_rep = jnp.concatenate(pieces, axis=0)
    for c in range(past // LANES):
        cs = slice(c * LANES, (c + 1) * LANES)
        s_sc[cs, :] = sc[cs, :] + b_rep[:, cs].T
    krow = lax.broadcasted_iota(jnp.int32, (LANES, LANES), 0)
    qcol = lax.broadcasted_iota(jnp.int32, (LANES, LANES), 1) % dec_seq
    bn = jnp.concatenate([expand(bn_ref[...]), jnp.zeros((LANES - dec_seq, LANES), F32)], axis=0)
    s_sc[past:, :] = jnp.where(krow <= qcol, sc[past:, :] + bn, -jnp.inf)

    m = jnp.max(s_sc[...], axis=0, keepdims=True)
    l = jnp.zeros((LANES, 1), F32)
    for c in range(n_keys // LANES):
        cs = slice(c * LANES, (c + 1) * LANES)
        pt = jnp.exp(s_sc[cs, :] - m).T
        l = l + jnp.sum(pt, -1, keepdims=True)
        p_sc[:, cs] = pt.astype(BF16)
    acc = jnp.dot(p_sc[...], vb_sc[...], preferred_element_type=F32)
    for h in range(n_heads):
        rs = slice(h * dec_seq, (h + 1) * dec_seq)
        hs = slice(h * HEAD_W, (h + 1) * HEAD_W)
        o = acc[rs, hs] / l[rs, :]
        o_ref[:, hs] = _head_rms(o) * gm_ref[:, hs]


def _sample_attention(page_table, qs, k_new, v_new, bias_past, bias_new, gm_b, cache_k, cache_v, *,
                      layer, dec_seq):
    n_seq, n_pages = page_table.shape
    depth, n_pool, page, n_heads, head_w = cache_k.shape
    d_b = n_heads * head_w
    past = n_pages * page
    cache_k = cache_k.reshape(depth, n_pool, page * n_heads, head_w)
    cache_v = cache_v.reshape(depth, n_pool, page * n_heads, head_w)

    new_rows = pl.BlockSpec((None, dec_seq * n_heads, head_w), lambda n, pt: (layer, n, 0))
    grid_spec = pltpu.PrefetchScalarGridSpec(
        num_scalar_prefetch=1,
        grid=(n_seq,),
        in_specs=[
            pl.BlockSpec((dec_seq, d_b), lambda n, pt: (n, 0)),
            new_rows,
            new_rows,
            pl.BlockSpec((None, n_heads, past), lambda n, pt: (n, 0, 0)),
            pl.BlockSpec((None, dec_seq, n_heads), lambda n, pt: (n, 0, 0)),
            pl.BlockSpec((1, d_b), lambda n, pt: (0, 0)),
            pl.BlockSpec(memory_space=pl.ANY),
            pl.BlockSpec(memory_space=pl.ANY),
        ],
        out_specs=pl.BlockSpec((dec_seq, d_b), lambda n, pt: (n, 0)),
        scratch_shapes=[
            pltpu.VMEM((2, past * n_heads, head_w), F32),
            pltpu.VMEM((2, past * n_heads, head_w), F32),
            pltpu.VMEM((past + LANES, d_b), BF16),
            pltpu.VMEM((past + LANES, d_b), BF16),
            pltpu.VMEM((past + LANES, LANES), F32),
            pltpu.VMEM((LANES, past + LANES), BF16),
            pltpu.SemaphoreType.DMA((2, 2)),
        ],
    )
    kern = functools.partial(_sample_attn_kernel, layer=layer, n_pages=n_pages, page=page, dec_seq=dec_seq)
    return pl.pallas_call(
        kern,
        grid_spec=grid_spec,
        out_shape=jax.ShapeDtypeStruct((n_seq * dec_seq, d_b), F32),
        compiler_params=_params(1, 56),
        name="sample_attn",
    )(page_table, qs, k_new, v_new, bias_past, bias_new, gm_b, cache_k, cache_v)


def _mix_kernel(ma_ref, mbp_ref, mbs_ref, mc_ref, x_ref, w_ref, g1_ref, b1_ref, wr2_ref, br_ref,
                x1g_ref, gid_ref, mixed_sc, *, n_prompt_tiles, alpha):
    i = pl.program_id(0)
    tm, d = x_ref.shape
    d_a = ma_ref.shape[-1]
    d_b = mbp_ref.shape[-1]
    mixed_sc[:, :d_a] = ma_ref[...]
    mixed_sc[:, d_a + d_b:] = mc_ref[...]

    @pl.when(i < n_prompt_tiles)
    def _():
        mixed_sc[:, d_a:d_a + d_b] = mbp_ref[...]

    @pl.when(i >= n_prompt_tiles)
    def _():
        mixed_sc[:, d_a:d_a + d_b] = mbs_ref[...].astype(BF16)

    y = jnp.dot(mixed_sc[...], w_ref[...], preferred_element_type=F32)
    x1 = _layer_norm(alpha * x_ref[...] + y, g1_ref[...], b1_ref[...])
    x1g_ref[:, :d] = x1

    hi = x1.astype(BF16)
    lo = (x1 - hi.astype(F32)).astype(BF16)
    both = jnp.dot(hi, wr2_ref[...], preferred_element_type=F32)
    logits = both[:, :LANES] + both[:, LANES:] + jnp.dot(lo, wr2_ref[:, :LANES], preferred_element_type=F32)
    score_all = jax.nn.sigmoid(logits.T)
    sel_all = score_all + br_ref[...]
    score = [score_all[e:e + 1, :] for e in range(N_EXPERTS)]
    sel = [sel_all[e:e + 1, :] for e in range(N_EXPERTS)]

    gscore = []
    for g in range(N_GROUPS):
        a0, a1, a2, a3 = sel[4 * g:4 * g + 4]
        h1, l1 = jnp.maximum(a0, a1), jnp.minimum(a0, a1)
        h2, l2 = jnp.maximum(a2, a3), jnp.minimum(a2, a3)
        gscore.append(jnp.maximum(h1, h2) + jnp.maximum(jnp.minimum(h1, h2), jnp.maximum(l1, l2)))
    gmax = functools.reduce(jnp.maximum, gscore)
    gid = jnp.full(gscore[0].shape, N_GROUPS - 1, jnp.int32)
    for g in range(N_GROUPS - 2, -1, -1):
        gid = jnp.where(gscore[g] == gmax, g, gid)
    gid_ref[...] = gid

    chosen = []
    for e in range(N_EXPERTS):
        g = e // EXPERTS_PER_GROUP
        rank = jnp.zeros(sel[e].shape, jnp.int32)
        for o in range(4 * g, 4 * g + 4):
            if o < e:
                rank = rank + (sel[o] >= sel[e]).astype(jnp.int32)
            elif o > e:
                rank = rank + (sel[o] > sel[e]).astype(jnp.int32)
        chosen.append(jnp.logical_and(gid == g, rank < 2))
    wsum = functools.reduce(jnp.add, [jnp.where(chosen[e], score[e], 0.0) for e in range(N_EXPERTS)])
    rowi = lax.broadcasted_iota(jnp.int32, (N_EXPERTS, tm), 0)
    gates_t = jnp.zeros((N_EXPERTS, tm), F32)
    for e in range(N_EXPERTS):
        gates_t = jnp.where(rowi == e, jnp.where(chosen[e], score[e] / wsum, 0.0), gates_t)
    gates_t = jnp.concatenate([gates_t, jnp.zeros((LANES - N_EXPERTS, tm), F32)], axis=0)
    x1g_ref[:, d:] = gates_t.T


def _mix(ma, mbp, mbs, mc, x, w_out, g1, b1, wr2, br_col, *, tm, n_prompt_rows, alpha):
    r, d = x.shape
    d_a, d_b, d_c = ma.shape[1], mbp.shape[1], mc.shape[1]
    npt = n_prompt_rows // tm
    rows = lambda i: (i, 0)
    return pl.pallas_call(
        functools.partial(_mix_kernel, n_prompt_tiles=npt, alpha=alpha),
        grid=(r // tm,),
        in_specs=[
            pl.BlockSpec((tm, d_a), rows),
            pl.BlockSpec((tm, d_b), lambda i: (jnp.minimum(i, npt - 1), 0)),
            pl.BlockSpec((tm, d_b), lambda i: (jnp.maximum(i - npt, 0), 0)),
            pl.BlockSpec((tm, d_c), rows),
            pl.BlockSpec((tm, d), rows),
            _resident((d_a + d_b + d_c, d)),
            _resident((1, d)),
            _resident((1, d)),
            _resident((d, 2 * LANES)),
            _resident((LANES, 1)),
        ],
        out_specs=[
            pl.BlockSpec((tm, d + LANES), rows),
            pl.BlockSpec((1, tm), lambda i: (0, i)),
        ],
        out_shape=[
            jax.ShapeDtypeStruct((r, d + LANES), F32),
            jax.ShapeDtypeStruct((1, r), jnp.int32),
        ],
        scratch_shapes=[pltpu.VMEM((tm, d_a + d_b + d_c), BF16)],
        compiler_params=_params(1, 40),
        name="mix",
    )(ma, mbp, mbs, mc, x, w_out, g1, b1, wr2, br_col)


def _moe_kernel(tok_ref, tg_ref, tr_ref, x1g_hbm, wg_ref, wu_ref, wd_ref, g2_ref, b2_ref, out_hbm,
                xg_sc, xb_sc, acc_sc, y_sc, sem, *, tm, d, n_rows, alpha):
    t = pl.program_id(0)
    j = pl.program_id(1)
    n_real = tr_ref[t]
    valid = n_real > 0
    slot = t % 2
    quarter = tm // EXPERTS_PER_GROUP

    def row_in(tile, r, sl):
        return pltpu.make_async_copy(x1g_hbm.at[pl.ds(tok_ref[tile * tm + r], 1), :],
                                     xg_sc.at[sl, pl.ds(r, 1), :], sem.at[sl])

    def wait_rows(sl):
        pltpu.make_async_copy(x1g_hbm.at[pl.ds(0, tm), :], xg_sc.at[sl], sem.at[sl]).wait()

    def row_out(tile, n_tile, r):
        dst = jnp.where(r < n_tile, tok_ref[tile * tm + r], n_rows + r)
        return pltpu.make_async_copy(y_sc.at[pl.ds(r, 1), :], out_hbm.at[pl.ds(dst, 1), :], sem.at[2])

    def wait_out():
        pltpu.make_async_copy(y_sc, out_hbm.at[pl.ds(0, tm), :], sem.at[2]).wait()

    def for_rows(n, fn, unroll):
        def body(r, carry):
            fn(r)
            return carry
        lax.fori_loop(0, n, body, 0, unroll=unroll)

    @pl.when(jnp.logical_and(j == 0, t == 0))
    def _():
        y_sc[...] = jnp.zeros_like(y_sc)
        spare = pltpu.make_async_copy(y_sc, out_hbm.at[pl.ds(n_rows, tm), :], sem.at[3])
        spare.start()
        spare.wait()
        for_rows(tm, lambda r: row_in(0, r, 0).start(), ROW_DMA_UNROLL)

    @pl.when(jnp.logical_and(valid, j == 0))
    def _():
        wait_rows(slot)
        xb_sc[...] = xg_sc[slot, :, :d].astype(BF16)
        acc_sc[...] = jnp.zeros_like(acc_sc)

    @pl.when(valid)
    def _():
        prev = jnp.maximum(t - 1, 0)
        n_prev = jnp.where(t > 0, tr_ref[prev], 0)
        for k in range(quarter):
            row_in(t + 1, j * quarter + k, 1 - slot).start()
            row_out(prev, n_prev, j * quarter + k).start()
        xb = xb_sc[...]
        hid = (jax.nn.silu(jnp.dot(xb, wg_ref[...].astype(BF16), preferred_element_type=F32))
               * jnp.dot(xb, wu_ref[...].astype(BF16), preferred_element_type=F32))
        lane = lax.broadcasted_iota(jnp.int32, (tm, LANES), 1)
        expert = tg_ref[t] * EXPERTS_PER_GROUP + j
        gate = jnp.sum(jnp.where(lane == expert, xg_sc[slot, :, d:], 0.0), -1, keepdims=True)
        acc_sc[...] += jnp.dot((hid * gate).astype(BF16), wd_ref[...].astype(BF16), preferred_element_type=F32)

    @pl.when(jnp.logical_and(valid, j == EXPERTS_PER_GROUP - 1))
    def _():
        wait_out()
        y_sc[...] = _layer_norm(alpha * xg_sc[slot, :, :d] + acc_sc[...], g2_ref[...], b2_ref[...])

        @pl.when(tr_ref[t + 1] == 0)
        def _():
            for_rows(tm, lambda r: row_out(t, n_real, r).start(), ROW_DMA_UNROLL)
            wait_out()
            wait_rows(1 - slot)


def _moe(token, tile_group, tile_rows, x1g, wg, wu, wd, g2, b2, *, layer, tm, n_rows, alpha):
    d = x1g.shape[1] - LANES
    d_ff = wg.shape[-1]
    n_tiles = tile_group.shape[0]

    def expert(t, j, tok, tg, tr):
        return layer, tg[t] * EXPERTS_PER_GROUP + jnp.where(tr[t] > 0, j, EXPERTS_PER_GROUP - 1), 0, 0

    grid_spec = pltpu.PrefetchScalarGridSpec(
        num_scalar_prefetch=3,
        grid=(n_tiles, EXPERTS_PER_GROUP),
        in_specs=[
            pl.BlockSpec(memory_space=pl.ANY),
            pl.BlockSpec((None, None, d, d_ff), expert),
            pl.BlockSpec((None, None, d, d_ff), expert),
            pl.BlockSpec((None, None, d_ff, d), expert),
            pl.BlockSpec((1, d), lambda *a: (0, 0)),
            pl.BlockSpec((1, d), lambda *a: (0, 0)),
        ],
        out_specs=pl.BlockSpec(memory_space=pl.ANY),
        scratch_shapes=[
            pltpu.VMEM((2, tm, d + LANES), F32),
            pltpu.VMEM((tm, d), BF16),
            pltpu.VMEM((tm, d), F32),
            pltpu.VMEM((tm, d), F32),
            pltpu.SemaphoreType.DMA((4,)),
        ],
    )
    return pl.pallas_call(
        functools.partial(_moe_kernel, tm=tm, d=d, n_rows=n_rows, alpha=alpha),
        grid_spec=grid_spec,
        out_shape=jax.ShapeDtypeStruct((n_rows + tm, d), F32),
        compiler_params=_params(2, 56),
        name="moe",
    )(token, tile_group, tile_rows, x1g, wg, wu, wd, g2, b2)


def _sort_by_group(gid, *, tm, n_tiles):
    r = gid.shape[0]
    onehot = (gid[:, None] == jnp.arange(N_GROUPS, dtype=jnp.int32)[None, :]).astype(jnp.int32)
    rank = jnp.sum((jnp.cumsum(onehot, axis=0) - onehot) * onehot, axis=1)
    counts = jnp.sum(onehot, axis=0)
    tiles = (counts + tm - 1) // tm
    tile_end = jnp.cumsum(tiles)
    tile_start = tile_end - tiles
    slot = tile_start[gid] * tm + rank
    token = jnp.zeros(((n_tiles + 1) * tm,), jnp.int32).at[slot].set(jnp.arange(r, dtype=jnp.int32))
    tile_ids = jnp.arange(n_tiles, dtype=jnp.int32)
    tile_group = jnp.minimum(jnp.sum(tile_ids[:, None] >= tile_end[None, :], axis=1), N_GROUPS - 1).astype(jnp.int32)
    tile_rows = jnp.clip(counts[tile_group] - (tile_ids - tile_start[tile_group]) * tm, 0, tm)
    tile_rows = jnp.where(tile_ids < tile_end[-1], tile_rows, 0).astype(jnp.int32)
    last_group = tile_group[jnp.maximum(tile_end[-1] - 1, 0)]
    tile_group = jnp.where(tile_rows > 0, tile_group, last_group)
    return token, tile_group, jnp.pad(tile_rows, (0, 1))


def _embed_kernel(x_ref, p_ref, wg_ref, bg_ref, wp_ref, *o_refs, n_prompt_tiles, split):
    x = x_ref[...]
    gate = jax.nn.sigmoid(jnp.dot(x.astype(BF16), wg_ref[...], preferred_element_type=F32) + bg_ref[...])
    proj = jnp.dot(p_ref[...].astype(BF16), wp_ref[...], preferred_element_type=F32)
    y = x + gate * proj
    if split:
        @pl.when(pl.program_id(0) < n_prompt_tiles)
        def _():
            o_refs[0][...] = y

        @pl.when(pl.program_id(0) >= n_prompt_tiles)
        def _():
            o_refs[1][...] = y
    else:
        o_refs[0][...] = y
        o_refs[1][...] = y.astype(BF16)


def _embed(x2, p, w_pg, b_pg, w_pp, *, tm, n_rows, n_prompt_rows, split):
    d = x2.shape[1]
    d_p = p.shape[1]
    npt = n_prompt_rows // tm
    rows = lambda i: (i, 0)
    if split:
        out_specs = [pl.BlockSpec((tm, d), lambda i: (jnp.minimum(i, npt - 1), 0)),
                     pl.BlockSpec((tm, d), lambda i: (jnp.maximum(i - npt, 0), 0))]
        out_shape = [jax.ShapeDtypeStruct((n_prompt_rows, d), F32),
                     jax.ShapeDtypeStruct((n_rows - n_prompt_rows, d), F32)]
    else:
        out_specs = [pl.BlockSpec((tm, d), rows), pl.BlockSpec((tm, d), rows)]
        out_shape = [jax.ShapeDtypeStruct((n_rows, d), F32), jax.ShapeDtypeStruct((n_rows, d), BF16)]
    return pl.pallas_call(
        functools.partial(_embed_kernel, n_prompt_tiles=npt, split=split),
        grid=(n_rows // tm,),
        in_specs=[
            pl.BlockSpec((tm, d), rows),
            pl.BlockSpec((tm, d_p), rows),
            _resident((d, d)),
            _resident((1, d)),
            _resident((d_p, d)),
        ],
        out_specs=out_specs,
        out_shape=out_shape,
        compiler_params=_params(1, 40),
        name="embed",
    )(x2, p, w_pg, b_pg, w_pp)


def kernel(x_prompt, x_sample, p_prompt, p_sample, cache_k, cache_v, cache_logf, state_conv, page_table, w_in, b_f, g_v, w_s, b_s, conv_w, g_mix, w_out, ln1_g, ln1_b, w_router, b_router, w_gate, w_up, w_down, ln2_g, ln2_b, w_ple_gate, b_ple_gate, w_ple_proj):
    depth = w_in.shape[0]
    batch, seq, d = x_prompt.shape
    n_seq, dec_seq, _ = x_sample.shape
    n_heads = cache_k.shape[3]
    d_a = g_v.shape[-1]
    d_b = n_heads * HEAD_W
    d_c = conv_w.shape[-1]
    np_rows = batch * seq
    ns_rows = n_seq * dec_seq
    n_rows = np_rows + ns_rows
    alpha = (2 * depth) ** 0.25
    tm = 256
    tm_moe = 512
    tq = 512
    assert batch == 1 and CONV_W - 1 <= dec_seq <= CHUNK and dec_seq & (dec_seq - 1) == 0
    assert np_rows % tq == 0 and ns_rows % tm == 0 and np_rows % tm_moe == 0 and ns_rows % tm_moe == 0
    assert w_s.shape[-1] == CHUNK and cache_k.shape[-1] == HEAD_W and n_heads <= SUBLANES
    n_sub = 4
    assert (n_seq * n_heads) % tm == 0 and (np_rows // tq) % n_sub == 0
    assert w_gate.shape[1] == N_EXPERTS and w_in.shape[-1] == 2 * d_a + 3 * d_b + n_heads + 3 * d_c
    n_tiles = n_rows // tm_moe + N_GROUPS

    x = jnp.concatenate([x_prompt.reshape(np_rows, d), x_sample.reshape(ns_rows, d)], axis=0)
    xb = x.astype(BF16)
    wr_pad = jnp.pad(w_router, ((0, 0), (0, LANES - N_EXPERTS)))
    wr_hi = wr_pad.astype(BF16)
    wr_lo = (wr_pad - wr_hi.astype(F32)).astype(BF16)
    wr2 = jnp.concatenate([wr_hi, wr_lo], axis=1)
    br_col = jnp.pad(b_router, (0, LANES - N_EXPERTS)).reshape(LANES, 1)
    reps = CHUNK // dec_seq

    k_prompt = jnp.zeros((depth, np_rows * n_heads, HEAD_W), F32)
    v_prompt = jnp.zeros((depth, np_rows * n_heads, HEAD_W), F32)
    k_sample = jnp.zeros((depth, ns_rows * n_heads, HEAD_W), F32)
    v_sample = jnp.zeros((depth, ns_rows * n_heads, HEAD_W), F32)
    outs = [[] for _ in range(5)]
    for i in range(depth):
        o_q, o_k, o_v = 2 * d_a, 2 * d_a + d_b, 2 * d_a + 2 * d_b
        o_f = 2 * d_a + 3 * d_b
        o_c = o_f + n_heads
        wi = w_in[i]
        w_a = wi[:, :o_q].astype(BF16)
        w_q = wi[:, o_q:o_k].astype(BF16)
        w_k = wi[:, o_k:o_v].astype(BF16)
        w_v = wi[:, o_v:o_f].astype(BF16)
        w_c = jnp.concatenate([wi[:, o_c:], jnp.pad(wi[:, o_f:o_c], ((0, 0), (0, LANES - n_heads)))], axis=1).astype(BF16)
        bf_pad = jnp.pad(b_f[i], (0, LANES - n_heads)).reshape(1, LANES)
        gm = g_mix[i].reshape(1, -1)
        gm_a, gm_b, gm_c = gm[:, :d_a], gm[:, d_a:d_a + d_b], gm[:, d_a + d_b:]
        ws_all = jnp.stack([w_s[i], jnp.tile(w_s[i][:, :dec_seq, :dec_seq], (1, reps, reps))])
        bs_rows = lambda b: jnp.repeat(b.T, HEAD_W, axis=1)
        bs_all = jnp.stack([bs_rows(b_s[i]), bs_rows(jnp.tile(b_s[i][:, :dec_seq], (1, reps)))])
        st = state_conv[i]
        zero_rows = jnp.zeros((n_seq, dec_seq - 1, d_c), F32)
        s1 = jnp.concatenate([st[:, 1:2], zero_rows], axis=1).reshape(ns_rows, d_c)
        s2 = jnp.concatenate([st[:, 0:1], st[:, 1:2], zero_rows[:, 1:]], axis=1).reshape(ns_rows, d_c)

        av_s, ma = _branch_a(xb, w_a, g_v[i].reshape(1, d_a), ws_all, bs_all, gm_a,
                             tm=tm, n_prompt_rows=np_rows, dec_seq=dec_seq)
        qt, qs = _project(xb, w_q, tm=tq, n_prompt_rows=np_rows, t_scale=ATTN_SCALE * LOG2E, sample_rows=True)
        kb, k_prompt, k_sample = _project(xb, w_k, tm=tq, n_prompt_rows=np_rows, flat=True,
                                          cache=(k_prompt, k_sample, i))
        vt, v_prompt, v_sample = _project(xb, w_v, tm=tq, n_prompt_rows=np_rows, t_scale=1.0,
                                          cache=(v_prompt, v_sample, i))
        zc, mc, lf, kbias = _branch_c(xb, w_c, bf_pad, conv_w[i], gm_c, s1, s2,
                                      tm=tm, n_prompt_rows=np_rows, dec_seq=dec_seq, d_b=d_b)
        mbp = _prompt_attention(qt, kb, kbias, vt, gm_b, n_prompt_rows=np_rows, tq=tq, n_sub=n_sub)

        lf_new = lf[np_rows:, :n_heads].reshape(n_seq, dec_seq, n_heads)
        pages_t = jnp.swapaxes(cache_logf[i], 1, 2)[page_table]
        lf_past_t = jnp.swapaxes(pages_t, 1, 2).reshape(n_seq, n_heads, -1)
        past = lf_past_t.shape[2]
        new_total = jnp.sum(lf_new, axis=1)
        after = jnp.arange(dec_seq)[None, :] > jnp.arange(dec_seq)[:, None]
        bias_new = jnp.sum(jnp.where(after[None, :, :, None], lf_new[:, None, :, :], 0.0), axis=2)
        bias_past = _suffix_bias(lf_past_t.reshape(n_seq * n_heads, past), new_total.reshape(-1, 1), tm=tm)
        mbs = _sample_attention(page_table, qs, k_sample, v_sample, bias_past.reshape(n_seq, n_heads, past),
                                bias_new, gm_b, cache_k, cache_v, layer=i, dec_seq=dec_seq)

        x1g, gid = _mix(ma, mbp, mbs, mc, x, w_out[i].astype(BF16), ln1_g[i].reshape(1, d), ln1_b[i].reshape(1, d),
                        wr2, br_col, tm=tm, n_prompt_rows=np_rows, alpha=alpha)
        token, tile_group, tile_rows = _sort_by_group(gid[0], tm=tm_moe, n_tiles=n_tiles)
        x2 = _moe(token, tile_group, tile_rows, x1g, w_gate, w_up, w_down,
                  ln2_g[i].reshape(1, d), ln2_b[i].reshape(1, d),
                  layer=i, tm=tm_moe, n_rows=n_rows, alpha=alpha)
        p = jnp.concatenate([p_prompt[i].reshape(np_rows, -1), p_sample[i].reshape(ns_rows, -1)], axis=0)
        last = i == depth - 1
        y = _embed(x2, p, w_ple_gate[i].astype(BF16), b_ple_gate[i].reshape(1, d), w_ple_proj[i].astype(BF16),
                   tm=tm, n_rows=n_rows, n_prompt_rows=np_rows, split=last)
        if not last:
            x, xb = y

        outs[0].append(lf[:np_rows, :n_heads].reshape(batch, seq, n_heads))
        outs[1].append(zc[np_rows - (CONV_W - 1):np_rows].reshape(batch, CONV_W - 1, d_c))
        outs[2].append(av_s.reshape(n_seq, dec_seq, d_a))
        outs[3].append(lf_new)
        outs[4].append(zc[np_rows:].reshape(n_seq, dec_seq, d_c)[:, dec_seq - (CONV_W - 1):])

    new_lf_p, new_conv_p, new_av_s, new_lf_s, new_conv_s = (jnp.stack(o) for o in outs)
    return (y[0].reshape(batch, seq, d), y[1].reshape(n_seq, dec_seq, d),
            k_prompt.reshape(depth, batch, seq, n_heads, HEAD_W), v_prompt.reshape(depth, batch, seq, n_heads, HEAD_W),
            new_lf_p, new_conv_p, new_av_s,
            k_sample.reshape(depth, n_seq, dec_seq, n_heads, HEAD_W),
            v_sample.reshape(depth, n_seq, dec_seq, n_heads, HEAD_W),
            new_lf_s, new_conv_s)
```
